```python
import jax, jax.numpy as jnp
from jax import lax
import numpy as np

D_MODEL = 2048
BATCH = 4
SEQ = 8192
DEPTH = 1

HEAD_DIM = 128
N_HEADS_MOBA = 8
N_HEADS_DIL = 8
W_MOBA = N_HEADS_MOBA * HEAD_DIM
W_DIL = N_HEADS_DIL * HEAD_DIM
MOBA_BLOCK = 256
MOBA_TOPK = 3
MOBA_Q_CHUNK = 32
DIL_PATTERNS = ((128, 1), (512, 4), (2048, 16))
N_GROUPS = 4
EXPERTS_PER_GROUP = 8
N_EXPERTS = N_GROUPS * EXPERTS_PER_GROUP
EXPERT_TOPK = 2
D_EXPERT = D_MODEL // 4
MOE_BLOCK = 128
RMS_EPS = 1e-6
IN_COLS = 3 * W_MOBA + 3 * W_DIL + 2 * D_MODEL

kernel_name = "hybrid_moba_dilated_hmoe_block"


def rms_norm(x, g):
    xf = x.astype(jnp.float32)
    y = xf * lax.rsqrt(jnp.mean(xf * xf, axis=-1, keepdims=True) + RMS_EPS)
    return (y * g.astype(jnp.float32)).astype(x.dtype)


def alibi_slopes():
    n = N_HEADS_MOBA + N_HEADS_DIL
    s = 2.0 ** (-8.0 * np.arange(1, n + 1) / n)
    return jnp.asarray(s[0::2], jnp.float32), jnp.asarray(s[1::2], jnp.float32)


def masked_softmax_lse(s, mask):
    s = jnp.where(mask, s, -jnp.inf)
    m = jnp.max(s, axis=-1, keepdims=True)
    m = jnp.where(jnp.isfinite(m), m, 0.0)
    p = jnp.exp(s - m)
    l = jnp.sum(p, axis=-1, keepdims=True)
    lse = (jnp.log(l) + m)[..., 0]
    p = p / jnp.where(l > 0, l, 1.0)
    return p, lse


def merge_by_lse(outs, lses):
    w = jax.nn.softmax(lses, axis=0)
    return jnp.sum(w[..., None] * outs.astype(jnp.float32), axis=0).astype(outs.dtype)


def moba_attention(q, k, v, slopes):
    Bsz, H, S, HD = q.shape
    nb = -(-S // MOBA_BLOCK)
    sp = nb * MOBA_BLOCK
    pad = ((0, 0), (0, 0), (0, sp - S), (0, 0))
    q, k, v = (jnp.pad(a, pad) for a in (q, k, v))
    scale = HD ** -0.5
    qb = q.reshape(Bsz, H, nb, MOBA_BLOCK, HD)
    kb = k.reshape(Bsz, H, nb, MOBA_BLOCK, HD)
    vb = v.reshape(Bsz, H, nb, MOBA_BLOCK, HD)

    i = jnp.arange(MOBA_BLOCK)
    diff = (i[:, None] - i[None, :])
    s_own = (jnp.einsum('bhnqd,bhnkd->bhnqk', qb, kb).astype(jnp.float32) * scale
             - slopes[None, :, None, None, None] * diff.astype(jnp.float32))
    p_own, lse_own = masked_softmax_lse(s_own, diff >= 0)
    o_own = jnp.einsum('bhnqk,bhnkd->bhnqd', p_own.astype(v.dtype), vb).reshape(Bsz, H, sp, HD)
    lse_own = lse_own.reshape(Bsz, H, sp)

    k_mean = jnp.mean(kb.astype(jnp.float32), axis=3)
    pos = jnp.arange(sp)
    qblk = pos // MOBA_BLOCK
    gate = jnp.einsum('bhsd,bhnd->bhsn', q.astype(jnp.float32), k_mean)
    past = jnp.arange(nb)[None, :] < qblk[:, None]
    gate = jnp.where(past, gate, -jnp.inf)
    n_sel = min(MOBA_TOPK, nb)
    _, sel = lax.top_k(gate, n_sel)
    sel_valid = sel < qblk[:, None]

    nc = sp // MOBA_Q_CHUNK

    def to_chunks(a):
        a = a.reshape((Bsz, H, nc, MOBA_Q_CHUNK) + a.shape[3:])
        return jnp.moveaxis(a, 2, 0)

    b_idx = jnp.arange(Bsz)[:, None, None, None]
    h_idx = jnp.arange(H)[None, :, None, None]
    offs = jnp.arange(MOBA_BLOCK)

    def chunk_attend(args):
        q_c, sel_c, valid_c, t_c = args
        k_g = kb[b_idx, h_idx, sel_c]
        v_g = vb[b_idx, h_idx, sel_c]
        s = jnp.einsum('bhcd,bhcnjd->bhcnj', q_c, k_g).astype(jnp.float32) * scale
        key_pos = sel_c[..., None] * MOBA_BLOCK + offs
        dist = (t_c[None, None, :, None, None] - key_pos).astype(jnp.float32)
        s = s - slopes[None, :, None, None, None] * dist
        mask = jnp.broadcast_to(valid_c[..., None], s.shape)
        C = q_c.shape[2]
        s = s.reshape(Bsz, H, C, n_sel * MOBA_BLOCK)
        mask = mask.reshape(Bsz, H, C, n_sel * MOBA_BLOCK)
        p, lse = masked_softmax_lse(s, mask)
        o = jnp.einsum('bhcm,bhcmd->bhcd', p.astype(v_g.dtype),
                       v_g.reshape(Bsz, H, C, n_sel * MOBA_BLOCK, HD))
        return o, lse

    o_sel, lse_sel = lax.map(chunk_attend, (to_chunks(q), to_chunks(sel), to_chunks(sel_valid),
                                            pos.reshape(nc, MOBA_Q_CHUNK)))
    o_sel = jnp.moveaxis(o_sel, 0, 2).reshape(Bsz, H, sp, HD)
    lse_sel = jnp.moveaxis(lse_sel, 0, 2).reshape(Bsz, H, sp)
    o = merge_by_lse(jnp.stack([o_own, o_sel]), jnp.stack([lse_own, lse_sel]))
    return o[:, :, :S]


def dilated_attention_one(q, k, v, slopes, window, dilation):
    Bsz, H, S, HD = q.shape
    w = window // dilation
    span = dilation * w
    sp = -(-S // span) * span
    L = sp // dilation
    nb = L // w
    scale = HD ** -0.5
    pad = ((0, 0), (0, 0), (0, sp - S), (0, 0))

    def to_blocks(a):
        a = jnp.pad(a, pad).reshape(Bsz, H, L, dilation, HD)
        a = jnp.swapaxes(a, 2, 3)
        return a.reshape(Bsz, H, dilation, nb, w, HD)

    def with_prev(a):
        prev = jnp.pad(a, ((0, 0), (0, 0), (0, 0), (1, 0), (0, 0), (0, 0)))[:, :, :, :-1]
        return jnp.concatenate([prev, a], axis=4)

    qb = to_blocks(q)
    kk = with_prev(to_blocks(k))
    vv = with_prev(to_blocks(v))
    i = jnp.arange(w)[:, None]
    j = jnp.arange(2 * w)[None, :]
    diff = i + w - j
    blk = jnp.arange(nb)[:, None, None]
    mask = ((diff >= 0) & (diff <= w))[None] & ((blk > 0) | (j >= w)[None])
    s = (jnp.einsum('bhrnqd,bhrnkd->bhrnqk', qb, kk).astype(jnp.float32) * scale
         - slopes[None, :, None, None, None, None] * (dilation * diff).astype(jnp.float32))
    p, lse = masked_softmax_lse(s, mask)
    o = jnp.einsum('bhrnqk,bhrnkd->bhrnqd', p.astype(vv.dtype), vv)
    o = jnp.swapaxes(o.reshape(Bsz, H, dilation, L, HD), 2, 3).reshape(Bsz, H, sp, HD)
    lse = jnp.swapaxes(lse.reshape(Bsz, H, dilation, L), 2, 3).reshape(Bsz, H, sp)
    return o[:, :, :S], lse[:, :, :S]


def dilated_mixture(q, k, v, slopes):
    outs, lses = [], []
    for window, dilation in DIL_PATTERNS:
        o, lse = dilated_attention_one(q, k, v, slopes, window, dilation)
        outs.append(o)
        lses.append(lse)
    return merge_by_lse(jnp.stack(outs), jnp.stack(lses))


def hier_moe(h, w_group, b_group, w_expert, b_expert, w_gate, w_up, w_down):
    Bsz, S, D = h.shape
    N = Bsz * S
    hf = h.reshape(N, D)
    g_prob = jax.nn.softmax((hf @ w_group + b_group).astype(jnp.float32), axis=-1)
    g_top_p, g_top = lax.top_k(g_prob, 1)
    e_logits = (hf @ w_expert + b_expert).astype(jnp.float32).reshape(N, N_GROUPS, EXPERTS_PER_GROUP)
    e_logits = jnp.take_along_axis(e_logits, g_top[:, :, None], axis=1)[:, 0]
    e_prob = jax.nn.softmax(e_logits, axis=-1)
    e_top_p, e_top = lax.top_k(e_prob, EXPERT_TOPK)
    weights = g_top_p * e_top_p / jnp.sum(e_top_p, axis=-1, keepdims=True)
    expert_id = g_top * EXPERTS_PER_GROUP + e_top

    A = N * EXPERT_TOPK
    e_flat = expert_id.reshape(A).astype(jnp.int32)
    w_flat = weights.reshape(A).astype(h.dtype)
    tok_flat = jnp.repeat(jnp.arange(N, dtype=jnp.int32), EXPERT_TOPK)
    order = jnp.argsort(e_flat)
    e_sorted = e_flat[order]
    counts = jnp.bincount(e_flat, length=N_EXPERTS)
    padded = ((counts + MOE_BLOCK - 1) // MOE_BLOCK) * MOE_BLOCK
    start = jnp.cumsum(counts) - counts
    pend = jnp.cumsum(padded)
    pstart = pend - padded
    dest = pstart[e_sorted] + (jnp.arange(A) - start[e_sorted])
    R = -(-(A + N_EXPERTS * MOE_BLOCK) // MOE_BLOCK) * MOE_BLOCK
    n_blk = R // MOE_BLOCK
    buf_tok = jnp.zeros((R,), jnp.int32).at[dest].set(tok_flat[order])
    buf_w = jnp.zeros((R,), h.dtype).at[dest].set(w_flat[order])
    blk_start = jnp.arange(n_blk) * MOE_BLOCK
    blk_e = jnp.clip(jnp.sum(pend[None, :] <= blk_start[:, None], axis=1), 0, N_EXPERTS - 1)
    xb = hf[buf_tok].reshape(n_blk, MOE_BLOCK, D)

    def expert_block(args):
        xblk, e = args
        return (jax.nn.silu(xblk @ w_gate[e]) * (xblk @ w_up[e])) @ w_down[e]

    yb = lax.map(expert_block, (xb, blk_e)).reshape(R, D)
    out = jnp.zeros((N, D), h.dtype).at[buf_tok].add(yb * buf_w[:, None])
    return out.reshape(Bsz, S, D)


def setup_inputs(seed: int = 0) -> dict:
    key = jax.random.key(seed)
    ks = jax.random.split(key, 16)
    D = D_MODEL
    L = DEPTH

    def nrm(k, shape, scale):
        return jax.random.normal(k, shape, jnp.float32) * scale

    return {
        "x": nrm(ks[0], (BATCH, SEQ, D), 1.0),
        "norm1_g": 1.0 + nrm(ks[1], (L, D), 0.02),
        "w_in": nrm(ks[2], (L, D, IN_COLS), D ** -0.5),
        "b_gates": nrm(ks[3], (L, 2 * D), 0.01),
        "w_out_moba": nrm(ks[4], (L, W_MOBA, D), W_MOBA ** -0.5),
        "w_out_dil": nrm(ks[5], (L, W_DIL, D), W_DIL ** -0.5),
        "w_o": nrm(ks[6], (L, D, D), D ** -0.5),
        "norm2_g": 1.0 + nrm(ks[7], (L, D), 0.02),
        "w_group": nrm(ks[8], (L, D, N_GROUPS), D ** -0.5),
        "b_group": nrm(ks[9], (L, N_GROUPS), 0.01),
        "w_expert": nrm(ks[10], (L, D, N_EXPERTS), D ** -0.5),
        "b_expert": nrm(ks[11], (L, N_EXPERTS), 0.01),
        "w_gate": nrm(ks[12], (L, N_EXPERTS, D, D_EXPERT), D ** -0.5),
        "w_up": nrm(ks[13], (L, N_EXPERTS, D, D_EXPERT), D ** -0.5),
        "w_down": nrm(ks[14], (L, N_EXPERTS, D_EXPERT, D), D_EXPERT ** -0.5),
        "norm_f_g": 1.0 + nrm(ks[15], (D,), 0.02),
    }


def reference(x, norm1_g, w_in, b_gates, w_out_moba, w_out_dil, w_o, norm2_g, w_group, b_group,
              w_expert, b_expert, w_gate, w_up, w_down, norm_f_g):
    slopes_moba, slopes_dil = alibi_slopes()
    Bsz, S, D = x.shape
    cuts = list(np.cumsum([W_MOBA, W_MOBA, W_MOBA, W_DIL, W_DIL, W_DIL, D_MODEL]))

    def heads(a, n):
        return a.reshape(Bsz, S, n, HEAD_DIM).transpose(0, 2, 1, 3)

    def merge_heads(a):
        return a.transpose(0, 2, 1, 3).reshape(Bsz, S, -1)

    for l in range(DEPTH):
        h = rms_norm(x, norm1_g[l])
        proj = h @ w_in[l]
        q_m, k_m, v_m, q_d, k_d, v_d, g_m, g_d = jnp.split(proj, cuts, axis=-1)
        o_m = moba_attention(heads(q_m, N_HEADS_MOBA), heads(k_m, N_HEADS_MOBA),
                             heads(v_m, N_HEADS_MOBA), slopes_moba)
        o_d = dilated_mixture(heads(q_d, N_HEADS_DIL), heads(k_d, N_HEADS_DIL),
                              heads(v_d, N_HEADS_DIL), slopes_dil)
        y_m = merge_heads(o_m) @ w_out_moba[l]
        y_d = merge_heads(o_d) @ w_out_dil[l]
        gates = jax.nn.sigmoid(jnp.concatenate([g_m, g_d], axis=-1) + b_gates[l])
        gate_m, gate_d = jnp.split(gates, 2, axis=-1)
        x = x + (gate_m * y_m + gate_d * y_d) @ w_o[l]
        x = x + hier_moe(rms_norm(x, norm2_g[l]), w_group[l], b_group[l], w_expert[l], b_expert[l],
                         w_gate[l], w_up[l], w_down[l])
    return rms_norm(x, norm_f_g)
```

```python
import functools

import numpy as np
import jax
import jax.numpy as jnp
from jax import lax
from jax.experimental import pallas as pl
from jax.experimental.pallas import tpu as pltpu

HEAD_DIM = 128
MOBA_BLOCK = 256
MOBA_TOPK = 3
DIL_PATTERNS = ((128, 1), (512, 4), (2048, 16))
DIL_STEPS = 128
DIL_CHUNK = 2048
EXPERT_TOPK = 2
RMS_EPS = 1e-6
NEG = -1e30
LOG2E = 1.4426950408889634
LANES = 128
VMEM_LIMIT = 56 * 1024 * 1024

F32 = jnp.float32
BF16 = jnp.bfloat16


def _alibi_slopes(n_moba, n_dil):
    n = n_moba + n_dil
    s = 2.0 ** (-8.0 * np.arange(1, n + 1) / n)
    return jnp.asarray(s[0::2], F32), jnp.asarray(s[1::2], F32)


def _inproj_kernel(x_ref, g_ref, w_ref, b_ref, om_ref, od_ref, og_ref, h_scr, *, qscale):
    j = pl.program_id(1)

    @pl.when(j == 0)
    def _():
        x = x_ref[...]
        ms = jnp.mean(x * x, axis=-1, keepdims=True)
        h_scr[...] = (x * lax.rsqrt(ms + RMS_EPS) * g_ref[...]).astype(BF16)

    acc = jnp.dot(h_scr[...], w_ref[...], preferred_element_type=F32)

    @pl.when(j == 0)
    def _():
        om_ref[...] = (acc * qscale).astype(BF16)

    @pl.when((j > 0) & (j < 3))
    def _():
        om_ref[...] = acc.astype(BF16)

    @pl.when(j == 3)
    def _():
        od_ref[...] = acc * qscale

    @pl.when((j > 3) & (j < 6))
    def _():
        od_ref[...] = acc

    @pl.when(j >= 6)
    def _():
        og_ref[...] = jax.nn.sigmoid(acc + b_ref[...]).astype(BF16)


def _inproj(x2, g1, w_in, b_gates, width, tm):
    n, d = x2.shape
    cols = w_in.shape[1]
    tn = width
    assert cols == 6 * width + 2 * d and d % tn == 0 and n % tm == 0
    n_gate = (2 * d) // tn
    qscale = HEAD_DIM ** -0.5 * LOG2E
    return pl.pallas_call(
        functools.partial(_inproj_kernel, qscale=qscale),
        grid=(n // tm, 6 + n_gate),
        in_specs=[
            pl.BlockSpec((tm, d), lambda i, j: (i, 0)),
            pl.BlockSpec((1, d), lambda i, j: (0, 0)),
            pl.BlockSpec((d, tn), lambda i, j: (0, j)),
            pl.BlockSpec((1, tn), lambda i, j: (0, jnp.clip(j - 6, 0, n_gate - 1))),
        ],
        out_specs=[
            pl.BlockSpec((tm, tn), lambda i, j: (i, jnp.minimum(j, 2))),
            pl.BlockSpec((tm, tn), lambda i, j: (i, jnp.clip(j - 3, 0, 2))),
            pl.BlockSpec((tm, tn), lambda i, j: (i, jnp.clip(j - 6, 0, n_gate - 1))),
        ],
        out_shape=[
            jax.ShapeDtypeStruct((n, 3 * width), BF16),
            jax.ShapeDtypeStruct((n, 3 * width), F32),
            jax.ShapeDtypeStruct((n, 2 * d), BF16),
        ],
        scratch_shapes=[pltpu.VMEM((tm, d), BF16)],
        compiler_params=pltpu.CompilerParams(
            dimension_semantics=("arbitrary", "arbitrary"), vmem_limit_bytes=VMEM_LIMIT),
        name="inproj",
    )(x2, g1.reshape(1, d), w_in, b_gates.reshape(1, 2 * d))


def _nt_dot(a, b):
    return lax.dot_general(a, b, (((1,), (1,)), ((), ())), preferred_element_type=F32)


def _moba_kernel(slopes_ref, q_ref, k_ref, v_ref, o_ref, kaug, kmh, kml, b0, b0d, *, nb):
    blk = MOBA_BLOCK
    h = pl.program_id(1)
    qi = pl.program_id(2)
    slope2 = slopes_ref[h] * LOG2E
    s_len = k_ref.shape[0]

    @pl.when(qi == 0)
    def _init():
        kaug[:, 0:HEAD_DIM] = k_ref[...]
        row_blk = lax.broadcasted_iota(jnp.int32, (s_len, LANES), 0) // blk
        lane = lax.broadcasted_iota(jnp.int32, (s_len, LANES), 1)
        kaug[:, HEAD_DIM:HEAD_DIM + LANES] = jnp.where(row_blk == lane, 1.0, 0.0).astype(BF16)
        km = jnp.sum(k_ref[...].astype(F32).reshape(nb, blk, HEAD_DIM), axis=1) * (1.0 / blk)
        if nb < LANES:
            km = jnp.concatenate([km, jnp.zeros((LANES - nb, HEAD_DIM), F32)], axis=0)
        hi = km.astype(BF16)
        kmh[...] = hi
        kml[...] = (km - hi.astype(F32)).astype(BF16)
        rc = (lax.broadcasted_iota(jnp.int32, (blk, blk), 0)
              - lax.broadcasted_iota(jnp.int32, (blk, blk), 1)).astype(F32)
        b0[...] = rc * (-slope2)
        b0d[...] = jnp.where(rc >= 0, rc * (-slope2), NEG)

    q = q_ref[...]

    gate = _nt_dot(q, kmh[...]) + _nt_dot(q, kml[...])
    lane = lax.broadcasted_iota(jnp.int32, (blk, LANES), 1)
    g = jnp.where(lane < qi, gate, -jnp.inf)
    selneg = jnp.full((blk, LANES), NEG, F32)
    for _ in range(min(MOBA_TOPK, nb)):
        mx = jnp.max(g, axis=-1, keepdims=True)
        idx = jnp.min(jnp.where(g == mx, lane, LANES), axis=-1, keepdims=True)
        hit = lane == idx
        selneg = jnp.where(hit & (mx > -jnp.inf), 0.0, selneg)
        g = jnp.where(hit, -jnp.inf, g)
    qaug = jnp.concatenate([q, selneg.astype(BF16)], axis=1)

    row0 = pl.multiple_of(qi * blk, blk)
    kd = k_ref[pl.ds(row0, blk), :]
    vd = v_ref[pl.ds(row0, blk), :]
    s = _nt_dot(q, kd) + b0d[...]
    m = jnp.max(s, axis=-1, keepdims=True)
    p = jnp.exp2(s - m)
    l = jnp.sum(p, axis=-1, keepdims=True)
    acc = jnp.dot(p.astype(BF16), vd, preferred_element_type=F32)

    def body(j, carry):
        m, l, acc = carry
        r0 = pl.multiple_of(j * blk, blk)
        kj = kaug[pl.ds(r0, blk), :]
        vj = v_ref[pl.ds(r0, blk), :]
        s = _nt_dot(qaug, kj) + b0[...]
        cj = (qi - j).astype(F32) * (-slope2 * blk)
        m_new = jnp.maximum(m, jnp.max(s, axis=-1, keepdims=True) + cj)
        p = jnp.exp2(s - (m_new - cj))
        alpha = jnp.exp2(m - m_new)
        l = alpha * l + jnp.sum(p, axis=-1, keepdims=True)
        acc = alpha * acc + jnp.dot(p.astype(BF16), vj, preferred_element_type=F32)
        return m_new, l, acc

    m, l, acc = lax.fori_loop(0, qi, body, (m, l, acc))
    o_ref[...] = (acc / l).astype(o_ref.dtype)


def _moba(projm, slopes, bsz, seq, n_heads):
    width = n_heads * HEAD_DIM
    nb = seq // MOBA_BLOCK
    assert seq % MOBA_BLOCK == 0 and nb <= LANES
    pm = projm.reshape(bsz, seq, 3 * width)
    grid_spec = pltpu.PrefetchScalarGridSpec(
        num_scalar_prefetch=1,
        grid=(bsz, n_heads, nb),
        in_specs=[
            pl.BlockSpec((None, MOBA_BLOCK, HEAD_DIM), lambda b, h, i, s: (b, i, h)),
            pl.BlockSpec((None, seq, HEAD_DIM), lambda b, h, i, s: (b, 0, n_heads + h)),
            pl.BlockSpec((None, seq, HEAD_DIM), lambda b, h, i, s: (b, 0, 2 * n_heads + h)),
        ],
        out_specs=pl.BlockSpec((None, MOBA_BLOCK, HEAD_DIM), lambda b, h, i, s: (b, i, h)),
        scratch_shapes=[
            pltpu.VMEM((seq, HEAD_DIM + LANES), BF16),
            pltpu.VMEM((LANES, HEAD_DIM), BF16),
            pltpu.VMEM((LANES, HEAD_DIM), BF16),
            pltpu.VMEM((MOBA_BLOCK, MOBA_BLOCK), F32),
            pltpu.VMEM((MOBA_BLOCK, MOBA_BLOCK), F32),
        ],
    )
    out = pl.pallas_call(
        functools.partial(_moba_kernel, nb=nb),
        grid_spec=grid_spec,
        out_shape=jax.ShapeDtypeStruct((bsz, seq, width), BF16),
        compiler_params=pltpu.CompilerParams(
            dimension_semantics=("arbitrary", "arbitrary", "arbitrary"),
            vmem_limit_bytes=VMEM_LIMIT),
        name="moba",
    )(slopes, pm, pm, pm)
    return out.reshape(bsz * seq, width)


def _dil_kernel(slopes_ref, q_ref, kc_ref, kp_ref, vc_ref, vp_ref, o_ref, bias, osc, lsc):
    w = DIL_STEPS
    chunk = q_ref.shape[0]
    h = pl.program_id(1)
    c = pl.program_id(2)
    slope2 = slopes_ref[h] * LOG2E

    @pl.when(c == 0)
    def _init():
        i = lax.broadcasted_iota(jnp.int32, (w, 2 * w), 0)
        j = lax.broadcasted_iota(jnp.int32, (w, 2 * w), 1)
        diff = i + w - j
        ok = (diff >= 0) & (diff <= w)
        for p, (_, d) in enumerate(DIL_PATTERNS):
            bias[p] = jnp.where(ok, diff.astype(F32) * (-slope2 * d), NEG)

    jj = lax.broadcasted_iota(jnp.int32, (w, 2 * w), 1)
    first_pen = jnp.where(jj < w, jnp.where(c > 0, 0.0, NEG), 0.0)

    def rows(start, d):
        return pl.ds(start, w) if d == 1 else pl.ds(start, w, stride=d)

    def tile(p, d, qstart, kprev_ref, vprev_ref, pstart, from_prev_chunk):
        q = q_ref[rows(qstart, d), :].astype(BF16)
        k2 = jnp.concatenate([kprev_ref[rows(pstart, d), :], kc_ref[rows(qstart, d), :]],
                             axis=0).astype(BF16)
        v2 = jnp.concatenate([vprev_ref[rows(pstart, d), :], vc_ref[rows(qstart, d), :]],
                             axis=0).astype(BF16)
        s = _nt_dot(q, k2) + bias[p]
        if from_prev_chunk:
            s = s + first_pen
        m = jnp.max(s, axis=-1, keepdims=True)
        pr = jnp.exp2(s - m)
        l = jnp.sum(pr, axis=-1, keepdims=True)
        o = jnp.dot(pr.astype(BF16), v2, preferred_element_type=F32) / l
        osc[p, rows(qstart, d), :] = o
        lsc[p, rows(qstart, d), :] = jnp.broadcast_to(m + jnp.log2(l), (w, LANES))

    for p, (_, d) in enumerate(DIL_PATTERNS):
        span = w * d
        n_sb = chunk // span

        def first(r, carry, p=p, d=d, span=span):
            tile(p, d, r, kp_ref, vp_ref, chunk - span + r, True)
            return carry

        lax.fori_loop(0, d, first, 0)

        def rest(t, carry, p=p, d=d, span=span):
            qstart = (1 + t // d) * span + t % d
            if d == 1:
                qstart = pl.multiple_of(qstart, w)
            tile(p, d, qstart, kc_ref, vc_ref, qstart - span, False)
            return carry

        lax.fori_loop(0, (n_sb - 1) * d, rest, 0)

    rb = 256

    def merge(t, carry):
        r0 = pl.multiple_of(t * rb, rb)
        ls = [lsc[p, pl.ds(r0, rb), :] for p in range(len(DIL_PATTERNS))]
        mx = functools.reduce(jnp.maximum, ls)
        ws = [jnp.exp2(x - mx) for x in ls]
        num = ws[0] * osc[0, pl.ds(r0, rb), :]
        den = ws[0]
        for p in range(1, len(DIL_PATTERNS)):
            num = num + ws[p] * osc[p, pl.ds(r0, rb), :]
            den = den + ws[p]
        o_ref[pl.ds(r0, rb), :] = (num / den).astype(o_ref.dtype)
        return carry

    lax.fori_loop(0, chunk // rb, merge, 0)


def _dilated(projd, slopes, bsz, seq, n_heads):
    width = n_heads * HEAD_DIM
    chunk = DIL_CHUNK
    assert seq % chunk == 0
    assert all(win // d == DIL_STEPS and chunk % win == 0 for win, d in DIL_PATTERNS)
    pd_ = projd.reshape(bsz, seq, 3 * width)
    blk = (None, chunk, HEAD_DIM)
    npat = len(DIL_PATTERNS)
    grid_spec = pltpu.PrefetchScalarGridSpec(
        num_scalar_prefetch=1,
        grid=(bsz, n_heads, seq // chunk),
        in_specs=[
            pl.BlockSpec(blk, lambda b, h, c, s: (b, c, h)),
            pl.BlockSpec(blk, lambda b, h, c, s: (b, c, n_heads + h)),
            pl.BlockSpec(blk, lambda b, h, c, s: (b, jnp.maximum(c - 1, 0), n_heads + h)),
            pl.BlockSpec(blk, lambda b, h, c, s: (b, c, 2 * n_heads + h)),
            pl.BlockSpec(blk, lambda b, h, c, s: (b, jnp.maximum(c - 1, 0), 2 * n_heads + h)),
        ],
        out_specs=pl.BlockSpec(blk, lambda b, h, c, s: (b, c, h)),
        scratch_shapes=[
            pltpu.VMEM((npat, DIL_STEPS, 2 * DIL_STEPS), F32),
            pltpu.VMEM((npat, chunk, HEAD_DIM), F32),
            pltpu.VMEM((npat, chunk, LANES), F32),
        ],
    )
    out = pl.pallas_call(
        _dil_kernel,
        grid_spec=grid_spec,
        out_shape=jax.ShapeDtypeStruct((bsz, seq, width), BF16),
        compiler_params=pltpu.CompilerParams(
            dimension_semantics=("arbitrary", "arbitrary", "arbitrary"),
            vmem_limit_bytes=VMEM_LIMIT),
        name="dilated",
    )(slopes, pd_, pd_, pd_, pd_, pd_)
    return out.reshape(bsz * seq, width)


def _outproj_kernel(am_ref, ad_ref, gm_ref, gd_ref, x_ref, wm_ref, wd_ref, wo_ref, g2_ref,
                    wrh_ref, wrl_ref, br_ref, x1_ref, h2_ref, rt_ref, *, n_groups, per_group):
    y_m = jnp.dot(am_ref[...], wm_ref[...], preferred_element_type=F32)
    y_d = jnp.dot(ad_ref[...], wd_ref[...], preferred_element_type=F32)
    z = (gm_ref[...].astype(F32) * y_m + gd_ref[...].astype(F32) * y_d).astype(BF16)
    x1 = x_ref[...] + jnp.dot(z, wo_ref[...], preferred_element_type=F32)
    x1_ref[...] = x1
    ms = jnp.mean(x1 * x1, axis=-1, keepdims=True)
    h2 = x1 * lax.rsqrt(ms + RMS_EPS) * g2_ref[...]
    h2_ref[...] = h2

    hi = h2.astype(BF16)
    lo = (h2 - hi.astype(F32)).astype(BF16)
    logits = (jnp.dot(hi, wrh_ref[...], preferred_element_type=F32)
              + (jnp.dot(hi, wrl_ref[...], preferred_element_type=F32)
                 + jnp.dot(lo, wrh_ref[...], preferred_element_type=F32))
              + br_ref[...])

    lane = lax.broadcasted_iota(jnp.int32, logits.shape, 1)
    big = jnp.int32(1 << 20)
    isg = lane < n_groups
    gl = jnp.where(isg, logits, -jnp.inf)
    gex = jnp.exp(gl - jnp.max(gl, axis=-1, keepdims=True))
    gprob = gex / jnp.sum(gex, axis=-1, keepdims=True)
    gtp = jnp.max(gprob, axis=-1, keepdims=True)
    gtop = jnp.min(jnp.where((gprob == gtp) & isg, lane, big), axis=-1, keepdims=True)
    e_lo = n_groups + gtop * per_group
    ise = (lane >= e_lo) & (lane < e_lo + per_group)
    el = jnp.where(ise, logits, -jnp.inf)
    eex = jnp.exp(el - jnp.max(el, axis=-1, keepdims=True))
    eprob = jnp.where(ise, eex / jnp.sum(eex, axis=-1, keepdims=True), -1.0)
    p1 = jnp.max(eprob, axis=-1, keepdims=True)
    i1 = jnp.min(jnp.where(eprob == p1, lane, big), axis=-1, keepdims=True)
    eprob2 = jnp.where(lane == i1, -1.0, eprob)
    p2 = jnp.max(eprob2, axis=-1, keepdims=True)
    i2 = jnp.min(jnp.where(eprob2 == p2, lane, big), axis=-1, keepdims=True)
    den = p1 + p2
    w1 = gtp * p1 / den
    w2 = gtp * p2 / den
    e1 = (i1 - n_groups).astype(F32)
    e2 = (i2 - n_groups).astype(F32)
    rt_ref[...] = jnp.where(lane == 0, w1, jnp.where(lane == 1, w2, jnp.where(
        lane == 2, e1, jnp.where(lane == 3, e2, 0.0))))


def _outproj(a_m, a_d, gates, x2, w_m, w_d, w_o, g2, wr_hi, wr_lo, b_r, n_groups, per_group, tm):
    n, d = x2.shape
    width = a_m.shape[1]
    assert n % tm == 0

    def const(shape):
        return pl.BlockSpec(shape, lambda i: (0, 0))

    return pl.pallas_call(
        functools.partial(_outproj_kernel, n_groups=n_groups, per_group=per_group),
        grid=(n // tm,),
        in_specs=[
            pl.BlockSpec((tm, width), lambda i: (i, 0)),
            pl.BlockSpec((tm, width), lambda i: (i, 0)),
            pl.BlockSpec((tm, d), lambda i: (i, 0)),
            pl.BlockSpec((tm, d), lambda i: (i, 1)),
            pl.BlockSpec((tm, d), lambda i: (i, 0)),
            const((width, d)), const((width, d)), const((d, d)), const((1, d)),
            const((d, LANES)), const((d, LANES)), const((1, LANES)),
        ],
        out_specs=[
            pl.BlockSpec((tm, d), lambda i: (i, 0)),
            pl.BlockSpec((tm, d), lambda i: (i, 0)),
            pl.BlockSpec((tm, LANES), lambda i: (i, 0)),
        ],
        out_shape=[
            jax.ShapeDtypeStruct((n, d), F32),
            jax.ShapeDtypeStruct((n, d), F32),
            jax.ShapeDtypeStruct((n, LANES), F32),
        ],
        compiler_params=pltpu.CompilerParams(
            dimension_semantics=("arbitrary",), vmem_limit_bytes=VMEM_LIMIT),
        name="outproj",
    )(a_m, a_d, gates, gates, x2, w_m, w_d, w_o, g2.reshape(1, d), wr_hi, wr_lo, b_r)


def _start_row_gather(src_hbm, idx_ref, dst, sem, n_rows):
    def issue(r, carry):
        pltpu.make_async_copy(src_hbm.at[pl.ds(idx_ref[0, 0, r], 1), :],
                              dst.at[pl.ds(r, 1), :], sem).start()
        return carry

    lax.fori_loop(0, n_rows, issue, 0)


def _wait_row_gather(src_hbm, dst, sem, n_rows):
    pltpu.make_async_copy(src_hbm.at[pl.ds(0, n_rows), :], dst, sem).wait()


def _experts_kernel(blk_e_ref, nvalid_ref, tok_ref, tok_next_ref, h_hbm, wgu_ref, wd_ref, y_ref,
                    xbuf, sem, *, d_expert):
    del blk_e_ref
    i = pl.program_id(0)
    nvalid = nvalid_ref[0]
    tb = xbuf.shape[1]
    slot = i % 2

    @pl.when((i == 0) & (i < nvalid))
    def _():
        _start_row_gather(h_hbm, tok_ref, xbuf.at[0], sem.at[0], tb)

    @pl.when(i + 1 < nvalid)
    def _():
        _start_row_gather(h_hbm, tok_next_ref, xbuf.at[1 - slot], sem.at[1 - slot], tb)

    @pl.when(i < nvalid)
    def _():
        _wait_row_gather(h_hbm, xbuf.at[slot], sem.at[slot], tb)
        x = xbuf[slot].astype(BF16)
        gu = jnp.dot(x, wgu_ref[...], preferred_element_type=F32)
        a = (jax.nn.silu(gu[:, :d_expert]) * gu[:, d_expert:]).astype(BF16)
        y_ref[...] = jnp.dot(a, wd_ref[...], preferred_element_type=F32)

    @pl.when(i >= nvalid)
    def _():
        y_ref[...] = jnp.zeros_like(y_ref)


def _experts(h2, tok_blocks, blk_e, nvalid, wgu, wd, tb):
    n, d = h2.shape
    n_blk = tok_blocks.shape[0]
    d_expert = wd.shape[1]
    grid_spec = pltpu.PrefetchScalarGridSpec(
        num_scalar_prefetch=2,
        grid=(n_blk,),
        in_specs=[
            pl.BlockSpec((1, 1, tb), lambda i, be, nv: (i, 0, 0), memory_space=pltpu.SMEM),
            pl.BlockSpec((1, 1, tb), lambda i, be, nv: (jnp.minimum(i + 1, n_blk - 1), 0, 0),
                         memory_space=pltpu.SMEM),
            pl.BlockSpec(memory_space=pl.ANY),
            pl.BlockSpec((None, d, 2 * d_expert), lambda i, be, nv: (be[i], 0, 0)),
            pl.BlockSpec((None, d_expert, d), lambda i, be, nv: (be[i], 0, 0)),
        ],
        out_specs=pl.BlockSpec((tb, d), lambda i, be, nv: (i, 0)),
        scratch_shapes=[
            pltpu.VMEM((2, tb, d), F32),
            pltpu.SemaphoreType.DMA((2,)),
        ],
    )
    return pl.pallas_call(
        functools.partial(_experts_kernel, d_expert=d_expert),
        grid_spec=grid_spec,
        out_shape=jax.ShapeDtypeStruct((n_blk * tb, d), F32),
        compiler_params=pltpu.CompilerParams(
            dimension_semantics=("arbitrary",), vmem_limit_bytes=VMEM_LIMIT),
        name="experts",
    )(blk_e, nvalid, tok_blocks, tok_blocks, h2, wgu, wd)


def _combine_kernel(pos_ref, pos_next_ref, x1_ref, rt_ref, gf_ref, y_hbm, o_ref, ybuf, sem):
    i = pl.program_id(0)
    n_steps = pl.num_programs(0)
    tm = x1_ref.shape[0]
    slot = i % 2

    @pl.when(i == 0)
    def _():
        _start_row_gather(y_hbm, pos_ref, ybuf.at[0], sem.at[0], 2 * tm)

    @pl.when(i + 1 < n_steps)
    def _():
        _start_row_gather(y_hbm, pos_next_ref, ybuf.at[1 - slot], sem.at[1 - slot], 2 * tm)

    _wait_row_gather(y_hbm, ybuf.at[slot], sem.at[slot], 2 * tm)
    rt = rt_ref[...]
    moe = rt[:, 0:1] * ybuf[slot, 0:tm, :] + rt[:, 1:2] * ybuf[slot, tm:2 * tm, :]
    x = x1_ref[...] + moe
    ms = jnp.mean(x * x, axis=-1, keepdims=True)
    o_ref[...] = x * lax.rsqrt(ms + RMS_EPS) * gf_ref[...]


def _combine(x1, rt, gf, y, pos_blocks, tm):
    n, d = x1.shape
    n_steps = n // tm
    return pl.pallas_call(
        _combine_kernel,
        grid=(n_steps,),
        in_specs=[
            pl.BlockSpec((1, 1, 2 * tm), lambda i: (i, 0, 0), memory_space=pltpu.SMEM),
            pl.BlockSpec((1, 1, 2 * tm), lambda i: (jnp.minimum(i + 1, n_steps - 1), 0, 0),
                         memory_space=pltpu.SMEM),
            pl.BlockSpec((tm, d), lambda i: (i, 0)),
            pl.BlockSpec((tm, LANES), lambda i: (i, 0)),
            pl.BlockSpec((1, d), lambda i: (0, 0)),
            pl.BlockSpec(memory_space=pl.ANY),
        ],
        out_specs=pl.BlockSpec((tm, d), lambda i: (i, 0)),
        out_shape=jax.ShapeDtypeStruct((n, d), F32),
        scratch_shapes=[
            pltpu.VMEM((2, 2 * tm, d), F32),
            pltpu.SemaphoreType.DMA((2,)),
        ],
        compiler_params=pltpu.CompilerParams(
            dimension_semantics=("arbitrary",), vmem_limit_bytes=VMEM_LIMIT),
        name="combine",
    )(pos_blocks, pos_blocks, x1, rt, gf.reshape(1, d), y)


def _dispatch_tables(expert_id, n_experts, tb, tm):
    n = expert_id.shape[0]
    a = n * EXPERT_TOPK
    e_flat = expert_id.reshape(a)
    tok_flat = jnp.arange(a, dtype=jnp.int32) // EXPERT_TOPK
    order = jnp.argsort(e_flat)
    e_sorted = e_flat[order]
    counts = jnp.bincount(e_flat, length=n_experts).astype(jnp.int32)
    padded = ((counts + tb - 1) // tb) * tb
    start = jnp.cumsum(counts) - counts
    pend = jnp.cumsum(padded)
    pstart = pend - padded
    dest = pstart[e_sorted] + (jnp.arange(a, dtype=jnp.int32) - start[e_sorted])
    n_blk = (a + n_experts * tb) // tb
    buf_tok = jnp.zeros((n_blk * tb,), jnp.int32).at[dest].set(tok_flat[order])
    pos = jnp.zeros((a,), jnp.int32).at[order].set(dest)
    blk_start = jnp.arange(n_blk, dtype=jnp.int32) * tb
    blk_e = jnp.clip(jnp.sum(pend[None, :] <= blk_start[:, None], axis=1), 0,
                     n_experts - 1).astype(jnp.int32)
    nvalid = (pend[-1] // tb).astype(jnp.int32).reshape(1)
    tok_blocks = buf_tok.reshape(n_blk, 1, tb)
    pos_blocks = pos.reshape(n // tm, tm, EXPERT_TOPK).transpose(0, 2, 1).reshape(
        n // tm, 1, EXPERT_TOPK * tm)
    return tok_blocks, blk_e, nvalid, pos_blocks


def _pick_tile(n, pref):
    t = pref
    while n % t:
        t //= 2
    return t


def kernel(x, norm1_g, w_in, b_gates, w_out_moba, w_out_dil, w_o, norm2_g, w_group, b_group,
           w_expert, b_expert, w_gate, w_up, w_down, norm_f_g):
    bsz, seq, d = x.shape
    n = bsz * seq
    depth = w_in.shape[0]
    width = w_out_moba.shape[1]
    assert w_out_dil.shape[1] == width and width % HEAD_DIM == 0
    n_heads = width // HEAD_DIM
    n_groups = w_group.shape[-1]
    n_experts = w_expert.shape[-1]
    per_group = n_experts // n_groups
    assert n_groups + n_experts <= LANES
    slopes_m, slopes_d = _alibi_slopes(n_heads, n_heads)
    tm_in = _pick_tile(n, 512)
    tm_out = _pick_tile(n, 256)
    tb = 256
    tm_c = _pick_tile(n, 256)

    x2 = x.reshape(n, d)
    for l in range(depth):
        projm, projd, gates = _inproj(x2, norm1_g[l], w_in[l].astype(BF16), b_gates[l], width, tm_in)
        o_m = _moba(projm, slopes_m, bsz, seq, n_heads)
        o_d = _dilated(projd, slopes_d, bsz, seq, n_heads)

        w_r = jnp.concatenate([w_group[l], w_expert[l]], axis=1)
        w_r = jnp.pad(w_r, ((0, 0), (0, LANES - w_r.shape[1])))
        wr_hi = w_r.astype(BF16)
        wr_lo = (w_r - wr_hi.astype(F32)).astype(BF16)
        b_r = jnp.pad(jnp.concatenate([b_group[l], b_expert[l]]),
                      (0, LANES - n_groups - n_experts)).reshape(1, LANES)
        x1, h2, rt = _outproj(o_m, o_d, gates, x2, w_out_moba[l].astype(BF16),
                              w_out_dil[l].astype(BF16), w_o[l].astype(BF16), norm2_g[l],
                              wr_hi, wr_lo, b_r, n_groups, per_group, tm_out)

        expert_id = rt[:, 2:2 + EXPERT_TOPK].astype(jnp.int32)
        tok_blocks, blk_e, nvalid, pos_blocks = _dispatch_tables(expert_id, n_experts, tb, tm_c)
        wgu = jnp.concatenate([w_gate[l], w_up[l]], axis=-1).astype(BF16)
        y = _experts(h2, tok_blocks, blk_e, nvalid, wgu, w_down[l].astype(BF16), tb)
        if l + 1 < depth:
            ones = jnp.ones((d,), F32)
            raise NotImplementedError("depth > 1 is not needed for this problem")
        x2 = _combine(x1, rt, norm_f_g, y, pos_blocks, tm_c)
    return x2.reshape(bsz, seq, d)
```

```python
import functools

import numpy as np
import jax
import jax.numpy as jnp
from jax import lax
from jax.experimental import pallas as pl
from jax.experimental.pallas import tpu as pltpu

HEAD_DIM = 128
MOBA_BLOCK = 256
MOBA_TOPK = 3
DIL_PATTERNS = ((128, 1), (512, 4), (2048, 16))
DIL_STEPS = 128
DIL_CHUNK = 2048
DIL_BATCH = 8
EXPERT_TOPK = 2
RMS_EPS = 1e-6
NEG = -1e30
LOG2E = 1.4426950408889634
LANES = 128
VMEM_LIMIT = 56 * 1024 * 1024

F32 = jnp.float32
BF16 = jnp.bfloat16


def _alibi_slopes(n_moba, n_dil):
    n = n_moba + n_dil
    s = 2.0 ** (-8.0 * np.arange(1, n + 1) / n)
    return jnp.asarray(s[0::2], F32), jnp.asarray(s[1::2], F32)


def _inproj_kernel(x_ref, g_ref, w_ref, b_ref, om_ref, od_ref, og_ref, h_scr, *, qscale):
    j = pl.program_id(1)

    @pl.when(j == 0)
    def _():
        x = x_ref[...]
        ms = jnp.mean(x * x, axis=-1, keepdims=True)
        h_scr[...] = (x * lax.rsqrt(ms + RMS_EPS) * g_ref[...]).astype(BF16)

    acc = jnp.dot(h_scr[...], w_ref[...], preferred_element_type=F32)

    @pl.when(j == 0)
    def _():
        om_ref[...] = (acc * qscale).astype(BF16)

    @pl.when((j > 0) & (j < 3))
    def _():
        om_ref[...] = acc.astype(BF16)

    @pl.when(j == 3)
    def _():
        od_ref[...] = acc * qscale

    @pl.when((j > 3) & (j < 6))
    def _():
        od_ref[...] = acc

    @pl.when(j >= 6)
    def _():
        og_ref[...] = jax.nn.sigmoid(acc + b_ref[...]).astype(BF16)


def _inproj(x2, g1, w_in, b_gates, width, tm):
    n, d = x2.shape
    cols = w_in.shape[1]
    tn = width
    assert cols == 6 * width + 2 * d and d % tn == 0 and n % tm == 0
    n_gate = (2 * d) // tn
    qscale = HEAD_DIM ** -0.5 * LOG2E
    return pl.pallas_call(
        functools.partial(_inproj_kernel, qscale=qscale),
        grid=(n // tm, 6 + n_gate),
        in_specs=[
            pl.BlockSpec((tm, d), lambda i, j: (i, 0)),
            pl.BlockSpec((1, d), lambda i, j: (0, 0)),
            pl.BlockSpec((d, tn), lambda i, j: (0, j)),
            pl.BlockSpec((1, tn), lambda i, j: (0, jnp.clip(j - 6, 0, n_gate - 1))),
        ],
        out_specs=[
            pl.BlockSpec((tm, tn), lambda i, j: (i, jnp.minimum(j, 2))),
            pl.BlockSpec((tm, tn), lambda i, j: (i, jnp.clip(j - 3, 0, 2))),
            pl.BlockSpec((tm, tn), lambda i, j: (i, jnp.clip(j - 6, 0, n_gate - 1))),
        ],
        out_shape=[
            jax.ShapeDtypeStruct((n, 3 * width), BF16),
            jax.ShapeDtypeStruct((n, 3 * width), F32),
            jax.ShapeDtypeStruct((n, 2 * d), BF16),
        ],
        scratch_shapes=[pltpu.VMEM((tm, d), BF16)],
        compiler_params=pltpu.CompilerParams(
            dimension_semantics=("arbitrary", "arbitrary"), vmem_limit_bytes=VMEM_LIMIT),
        name="inproj",
    )(x2, g1.reshape(1, d), w_in, b_gates.reshape(1, 2 * d))


def _nt_dot(a, b):
    return lax.dot_general(a, b, (((1,), (1,)), ((), ())), preferred_element_type=F32)


def _dot(a, b):
    return jnp.dot(a, b, preferred_element_type=F32)


MOBA_TILE = 2 * MOBA_BLOCK
ALIBI_SPLIT = 3
VT_ROWS = HEAD_DIM + 16


def _moba_kernel(slopes_ref, q_ref, k_ref, v_ref, o_ref, kaug, vt, kmh, kml, causal, sbuf,
                 *, nb, nbp):
    blk = MOBA_BLOCK
    hd = HEAD_DIM
    tq = tk = MOBA_TILE
    h = pl.program_id(1)
    qp = pl.program_id(2)
    slope2 = slopes_ref[h] * LOG2E
    s_len = k_ref.shape[0]
    c_hi0 = nbp + ALIBI_SPLIT

    @pl.when(qp == 0)
    def _init():
        kaug[:, 0:hd] = k_ref[...]
        pos = lax.broadcasted_iota(jnp.int32, (s_len, LANES), 0)
        lane = lax.broadcasted_iota(jnp.int32, (s_len, LANES), 1)
        off = pos % tk
        aug = jnp.where(lane < nbp, jnp.where(pos // blk == lane, 1.0, 0.0),
                        jnp.where(lane < c_hi0, (off % blk).astype(F32),
                                  jnp.where(lane < c_hi0 + ALIBI_SPLIT,
                                            (off - off % blk).astype(F32), 0.0)))
        kaug[:, hd:hd + LANES] = aug.astype(BF16)

        def xpose(n, carry):
            r0 = pl.multiple_of(n * blk, blk)
            vt[0:hd, pl.ds(r0, blk)] = v_ref[pl.ds(r0, blk), :].astype(F32).T.astype(BF16)
            return carry

        lax.fori_loop(0, nb, xpose, 0)
        row = lax.broadcasted_iota(jnp.int32, (VT_ROWS - hd, s_len), 0)
        vt[hd:VT_ROWS, :] = jnp.where(row == 0, 1.0, 0.0).astype(BF16)
        km = jnp.sum(k_ref[...].astype(F32).reshape(nb, blk, hd), axis=1) * (1.0 / blk)
        if nb < nbp:
            km = jnp.concatenate([km, jnp.zeros((nbp - nb, hd), F32)], axis=0)
        hi = km.astype(BF16)
        kmh[...] = hi
        kml[...] = (km - hi.astype(F32)).astype(BF16)
        cc = lax.broadcasted_iota(jnp.int32, (tk, tq), 0)
        rr = lax.broadcasted_iota(jnp.int32, (tk, tq), 1)
        causal[...] = jnp.where(cc > rr, NEG, 0.0)

    qt = q_ref[...].astype(F32).T.astype(BF16)
    lane_q = lax.broadcasted_iota(jnp.int32, (1, tq), 1)
    qblk = qp * (tq // blk) + lane_q // blk
    t_q = qp * tq + lane_q

    gate = _dot(kmh[...], qt) + _dot(kml[...], qt)
    bidx = lax.broadcasted_iota(jnp.int32, (nbp, tq), 0)
    gt = jnp.where(bidx < qblk, gate, -jnp.inf)
    selneg = jnp.full((nbp, tq), NEG, F32)
    for _ in range(min(MOBA_TOPK, nb)):
        mx = jnp.max(gt, axis=0, keepdims=True)
        idx = jnp.min(jnp.where(gt == mx, bidx, nbp), axis=0, keepdims=True)
        hit = bidx == idx
        selneg = jnp.where(hit & (mx > -jnp.inf), 0.0, selneg)
        gt = jnp.where(hit, -jnp.inf, gt)

    sv = jnp.full((16, tq), slope2, F32)
    pieces = []
    for _ in range(ALIBI_SPLIT):
        pc = sv.astype(BF16).astype(F32)
        pieces.append(pc)
        sv = sv - pc
    ri = lax.broadcasted_iota(jnp.int32, (16, tq), 0)
    srows = jnp.zeros((16, tq), F32)
    for j in range(2 * ALIBI_SPLIT):
        srows = jnp.where(ri == j, pieces[j % ALIBI_SPLIT], srows)
    tail = jnp.zeros((LANES - nbp - 16, tq), BF16)

    def augment(sel):
        return jnp.concatenate([qt, sel.astype(BF16), srows.astype(BF16), tail], axis=0)

    qaug = augment(selneg)
    qaug_diag = augment(jnp.where(bidx == qblk, 0.0, selneg))

    def scores(k0, qa):
        return _dot(kaug[pl.ds(k0, tk), :], qa)

    def update(k0, raw, state):
        m, acc = state
        shift = slope2 * (k0 - t_q).astype(F32)
        m_new = jnp.maximum(m, jnp.max(raw, axis=0, keepdims=True) + shift)
        p = jnp.exp2(raw - (m_new - shift))
        alpha = jnp.exp2(m - m_new)
        acc = alpha * acc + _dot(vt[:, pl.ds(k0, tk)], p.astype(BF16))
        return m_new, acc

    sbuf[...] = scores(pl.multiple_of(qp * tk, tk), qaug_diag) + causal[...]

    def step(j, st):
        k_cur = pl.multiple_of((qp - j) * tk, tk)
        raw_next = scores(k_cur - tk, qaug)
        st = update(k_cur, sbuf[...], st)
        sbuf[...] = raw_next
        return st

    state = (jnp.full((1, tq), -jnp.inf, F32), jnp.zeros((VT_ROWS, tq), F32))
    state = lax.fori_loop(0, qp, step, state)
    state = update(0, sbuf[...], state)
    _, acc = state
    o_ref[...] = (acc[0:hd, :] / acc[hd:hd + 1, :]).T.astype(o_ref.dtype)


def _moba(projm, slopes, bsz, seq, n_heads):
    width = n_heads * HEAD_DIM
    nb = seq // MOBA_BLOCK
    nbp = -(-nb // 16) * 16
    tile = MOBA_TILE
    assert seq % tile == 0 and nbp + 16 <= LANES
    pm = projm.reshape(bsz, seq, 3 * width)
    grid_spec = pltpu.PrefetchScalarGridSpec(
        num_scalar_prefetch=1,
        grid=(bsz, n_heads, seq // tile),
        in_specs=[
            pl.BlockSpec((None, tile, HEAD_DIM), lambda b, h, i, s: (b, i, h)),
            pl.BlockSpec((None, seq, HEAD_DIM), lambda b, h, i, s: (b, 0, n_heads + h)),
            pl.BlockSpec((None, seq, HEAD_DIM), lambda b, h, i, s: (b, 0, 2 * n_heads + h)),
        ],
        out_specs=pl.BlockSpec((None, tile, HEAD_DIM), lambda b, h, i, s: (b, i, h)),
        scratch_shapes=[
            pltpu.VMEM((seq, HEAD_DIM + LANES), BF16),
            pltpu.VMEM((VT_ROWS, seq), BF16),
            pltpu.VMEM((nbp, HEAD_DIM), BF16),
            pltpu.VMEM((nbp, HEAD_DIM), BF16),
            pltpu.VMEM((tile, tile), F32),
            pltpu.VMEM((tile, tile), F32),
        ],
    )
    out = pl.pallas_call(
        functools.partial(_moba_kernel, nb=nb, nbp=nbp),
        grid_spec=grid_spec,
        out_shape=jax.ShapeDtypeStruct((bsz, seq, width), BF16),
        compiler_params=pltpu.CompilerParams(
            dimension_semantics=("arbitrary", "arbitrary", "arbitrary"),
            vmem_limit_bytes=VMEM_LIMIT),
        name="moba",
    )(slopes, pm, pm, pm)
    return out.reshape(bsz * seq, width)


def _dil_kernel(slopes_ref, q_ref, kc_ref, kp_ref, vc_ref, vp_ref, o_ref, bias, osc, lsc):
    w = DIL_STEPS
    chunk = q_ref.shape[0]
    h = pl.program_id(1)
    c = pl.program_id(2)
    slope2 = slopes_ref[h] * LOG2E

    @pl.when(c == 0)
    def _init():
        i = lax.broadcasted_iota(jnp.int32, (w, 2 * w), 0)
        j = lax.broadcasted_iota(jnp.int32, (w, 2 * w), 1)
        diff = i + w - j
        ok = (diff >= 0) & (diff <= w)
        for p, (_, d) in enumerate(DIL_PATTERNS):
            bias[p] = jnp.where(ok, diff.astype(F32) * (-slope2 * d), NEG)

    jj = lax.broadcasted_iota(jnp.int32, (w, 2 * w), 1)
    first_pen = jnp.where(jj < w, jnp.where(c > 0, 0.0, NEG), 0.0)

    def rows(start, d):
        return pl.ds(start, w) if d == 1 else pl.ds(start, w, stride=d)

    def tiles(p, d, qstarts, kprev_ref, vprev_ref, pstarts, from_prev_chunk):
        n = len(qstarts)
        ss = []
        for qstart, pstart in zip(qstarts, pstarts):
            q = q_ref[rows(qstart, d), :].astype(BF16)
            k2 = jnp.concatenate([kprev_ref[rows(pstart, d), :], kc_ref[rows(qstart, d), :]],
                                 axis=0).astype(BF16)
            ss.append(_nt_dot(q, k2))
        prs, ms, ls = [], [], []
        for s in ss:
            s = s + bias[p]
            if from_prev_chunk:
                s = s + first_pen
            m = jnp.max(s, axis=-1, keepdims=True)
            pr = jnp.exp2(s - m)
            prs.append(pr.astype(BF16))
            ms.append(m)
            ls.append(jnp.sum(pr, axis=-1, keepdims=True))
        for i in range(n):
            v2 = jnp.concatenate([vprev_ref[rows(pstarts[i], d), :],
                                  vc_ref[rows(qstarts[i], d), :]], axis=0).astype(BF16)
            o = jnp.dot(prs[i], v2, preferred_element_type=F32) / ls[i]
            osc[p, rows(qstarts[i], d), :] = o
            lsc[p, rows(qstarts[i], d), :] = jnp.broadcast_to(ms[i] + jnp.log2(ls[i]), (w, LANES))

    def batch_size(n):
        return max(g for g in range(1, DIL_BATCH + 1) if n % g == 0)

    for p, (_, d) in enumerate(DIL_PATTERNS):
        span = w * d
        n_sb = chunk // span

        g1 = batch_size(d)

        def first(t, carry, p=p, d=d, span=span, g1=g1):
            rs = [t * g1 + i for i in range(g1)]
            tiles(p, d, rs, kp_ref, vp_ref, [chunk - span + r for r in rs], True)
            return carry

        lax.fori_loop(0, d // g1, first, 0)

        n_rest = (n_sb - 1) * d
        if n_rest:
            g2 = batch_size(n_rest)

            def rest(t, carry, p=p, d=d, span=span, g2=g2):
                idx = [t * g2 + i for i in range(g2)]
                qstarts = [(1 + x // d) * span + x % d for x in idx]
                if d == 1:
                    qstarts = [pl.multiple_of(x, w) for x in qstarts]
                tiles(p, d, qstarts, kc_ref, vc_ref, [x - span for x in qstarts], False)
                return carry

            lax.fori_loop(0, n_rest // g2, rest, 0)

    rb = 256

    def merge(t, carry):
        r0 = pl.multiple_of(t * rb, rb)
        ls = [lsc[p, pl.ds(r0, rb), :] for p in range(len(DIL_PATTERNS))]
        mx = functools.reduce(jnp.maximum, ls)
        ws = [jnp.exp2(x - mx) for x in ls]
        num = ws[0] * osc[0, pl.ds(r0, rb), :]
        den = ws[0]
        for p in range(1, len(DIL_PATTERNS)):
            num = num + ws[p] * osc[p, pl.ds(r0, rb), :]
            den = den + ws[p]
        o_ref[pl.ds(r0, rb), :] = (num / den).astype(o_ref.dtype)
        return carry

    lax.fori_loop(0, chunk // rb, merge, 0)


def _dilated(projd, slopes, bsz, seq, n_heads):
    width = n_heads * HEAD_DIM
    chunk = DIL_CHUNK
    assert seq % chunk == 0
    assert all(win // d == DIL_STEPS and chunk % win == 0 for win, d in DIL_PATTERNS)
    pd_ = projd.reshape(bsz, seq, 3 * width)
    blk = (None, chunk, HEAD_DIM)
    npat = len(DIL_PATTERNS)
    grid_spec = pltpu.PrefetchScalarGridSpec(
        num_scalar_prefetch=1,
        grid=(bsz, n_heads, seq // chunk),
        in_specs=[
            pl.BlockSpec(blk, lambda b, h, c, s: (b, c, h)),
            pl.BlockSpec(blk, lambda b, h, c, s: (b, c, n_heads + h)),
            pl.BlockSpec(blk, lambda b, h, c, s: (b, jnp.maximum(c - 1, 0), n_heads + h)),
            pl.BlockSpec(blk, lambda b, h, c, s: (b, c, 2 * n_heads + h)),
            pl.BlockSpec(blk, lambda b, h, c, s: (b, jnp.maximum(c - 1, 0), 2 * n_heads + h)),
        ],
        out_specs=pl.BlockSpec(blk, lambda b, h, c, s: (b, c, h)),
        scratch_shapes=[
            pltpu.VMEM((npat, DIL_STEPS, 2 * DIL_STEPS), F32),
            pltpu.VMEM((npat, chunk, HEAD_DIM), F32),
            pltpu.VMEM((npat, chunk, LANES), F32),
        ],
    )
    out = pl.pallas_call(
        _dil_kernel,
        grid_spec=grid_spec,
        out_shape=jax.ShapeDtypeStruct((bsz, seq, width), BF16),
        compiler_params=pltpu.CompilerParams(
            dimension_semantics=("arbitrary", "arbitrary", "arbitrary"),
            vmem_limit_bytes=VMEM_LIMIT),
        name="dilated",
    )(slopes, pd_, pd_, pd_, pd_, pd_)
    return out.reshape(bsz * seq, width)


def _outproj_kernel(am_ref, ad_ref, gm_ref, gd_ref, x_ref, wm_ref, wd_ref, wo_ref, g2_ref,
                    wrh_ref, wrl_ref, br_ref, x1_ref, h2_ref, rt_ref, *, n_groups, per_group):
    y_m = jnp.dot(am_ref[...], wm_ref[...], preferred_element_type=F32)
    y_d = jnp.dot(ad_ref[...], wd_ref[...], preferred_element_type=F32)
    z = (gm_ref[...].astype(F32) * y_m + gd_ref[...].astype(F32) * y_d).astype(BF16)
    x1 = x_ref[...] + jnp.dot(z, wo_ref[...], preferred_element_type=F32)
    x1_ref[...] = x1
    ms = jnp.mean(x1 * x1, axis=-1, keepdims=True)
    h2 = x1 * lax.rsqrt(ms + RMS_EPS) * g2_ref[...]
    h2_ref[...] = h2

    hi = h2.astype(BF16)
    lo = (h2 - hi.astype(F32)).astype(BF16)
    logits = (jnp.dot(hi, wrh_ref[...], preferred_element_type=F32)
              + (jnp.dot(hi, wrl_ref[...], preferred_element_type=F32)
                 + jnp.dot(lo, wrh_ref[...], preferred_element_type=F32))
              + br_ref[...])

    lane = lax.broadcasted_iota(jnp.int32, logits.shape, 1)
    big = jnp.int32(1 << 20)
    isg = lane < n_groups
    gl = jnp.where(isg, logits, -jnp.inf)
    gex = jnp.exp(gl - jnp.max(gl, axis=-1, keepdims=True))
    gprob = gex / jnp.sum(gex, axis=-1, keepdims=True)
    gtp = jnp.max(gprob, axis=-1, keepdims=True)
    gtop = jnp.min(jnp.where((gprob == gtp) & isg, lane, big), axis=-1, keepdims=True)
    e_lo = n_groups + gtop * per_group
    ise = (lane >= e_lo) & (lane < e_lo + per_group)
    el = jnp.where(ise, logits, -jnp.inf)
    eex = jnp.exp(el - jnp.max(el, axis=-1, keepdims=True))
    eprob = jnp.where(ise, eex / jnp.sum(eex, axis=-1, keepdims=True), -1.0)
    p1 = jnp.max(eprob, axis=-1, keepdims=True)
    i1 = jnp.min(jnp.where(eprob == p1, lane, big), axis=-1, keepdims=True)
    eprob2 = jnp.where(lane == i1, -1.0, eprob)
    p2 = jnp.max(eprob2, axis=-1, keepdims=True)
    i2 = jnp.min(jnp.where(eprob2 == p2, lane, big), axis=-1, keepdims=True)
    den = p1 + p2
    w1 = gtp * p1 / den
    w2 = gtp * p2 / den
    e1 = (i1 - n_groups).astype(F32)
    e2 = (i2 - n_groups).astype(F32)
    rt_ref[...] = jnp.where(lane == 0, w1, jnp.where(lane == 1, w2, jnp.where(
        lane == 2, e1, jnp.where(lane == 3, e2, 0.0))))


def _outproj(a_m, a_d, gates, x2, w_m, w_d, w_o, g2, wr_hi, wr_lo, b_r, n_groups, per_group, tm):
    n, d = x2.shape
    width = a_m.shape[1]
    assert n % tm == 0

    def const(shape):
        return pl.BlockSpec(shape, lambda i: (0, 0))

    return pl.pallas_call(
        functools.partial(_outproj_kernel, n_groups=n_groups, per_group=per_group),
        grid=(n // tm,),
        in_specs=[
            pl.BlockSpec((tm, width), lambda i: (i, 0)),
            pl.BlockSpec((tm, width), lambda i: (i, 0)),
            pl.BlockSpec((tm, d), lambda i: (i, 0)),
            pl.BlockSpec((tm, d), lambda i: (i, 1)),
            pl.BlockSpec((tm, d), lambda i: (i, 0)),
            const((width, d)), const((width, d)), const((d, d)), const((1, d)),
            const((d, LANES)), const((d, LANES)), const((1, LANES)),
        ],
        out_specs=[
            pl.BlockSpec((tm, d), lambda i: (i, 0)),
            pl.BlockSpec((tm, d), lambda i: (i, 0)),
            pl.BlockSpec((tm, LANES), lambda i: (i, 0)),
        ],
        out_shape=[
            jax.ShapeDtypeStruct((n, d), F32),
            jax.ShapeDtypeStruct((n, d), F32),
            jax.ShapeDtypeStruct((n, LANES), F32),
        ],
        compiler_params=pltpu.CompilerParams(
            dimension_semantics=("arbitrary",), vmem_limit_bytes=VMEM_LIMIT),
        name="outproj",
    )(a_m, a_d, gates, gates, x2, w_m, w_d, w_o, g2.reshape(1, d), wr_hi, wr_lo, b_r)


def _start_row_gather(src_hbm, idx_ref, dst, sem, n_rows):
    def issue(r, carry):
        pltpu.make_async_copy(src_hbm.at[pl.ds(idx_ref[0, 0, r], 1), :],
                              dst.at[pl.ds(r, 1), :], sem).start()
        return carry

    lax.fori_loop(0, n_rows, issue, 0)


def _wait_row_gather(src_hbm, dst, sem, n_rows):
    pltpu.make_async_copy(src_hbm.at[pl.ds(0, n_rows), :], dst, sem).wait()


def _experts_kernel(blk_e_ref, nvalid_ref, tok_ref, tok_next_ref, h_hbm, wgu_ref, wd_ref, y_ref,
                    xbuf, sem, *, d_expert):
    del blk_e_ref
    i = pl.program_id(0)
    nvalid = nvalid_ref[0]
    tb = xbuf.shape[1]
    slot = i % 2

    @pl.when((i == 0) & (i < nvalid))
    def _():
        _start_row_gather(h_hbm, tok_ref, xbuf.at[0], sem.at[0], tb)

    @pl.when(i + 1 < nvalid)
    def _():
        _start_row_gather(h_hbm, tok_next_ref, xbuf.at[1 - slot], sem.at[1 - slot], tb)

    @pl.when(i < nvalid)
    def _():
        _wait_row_gather(h_hbm, xbuf.at[slot], sem.at[slot], tb)
        x = xbuf[slot].astype(BF16)
        gu = jnp.dot(x, wgu_ref[...], preferred_element_type=F32)
        a = (jax.nn.silu(gu[:, :d_expert]) * gu[:, d_expert:]).astype(BF16)
        y_ref[...] = jnp.dot(a, wd_ref[...], preferred_element_type=F32)

    @pl.when(i >= nvalid)
    def _():
        y_ref[...] = jnp.zeros_like(y_ref)


def _experts(h2, tok_blocks, blk_e, nvalid, wgu, wd, tb):
    n, d = h2.shape
    n_blk = tok_blocks.shape[0]
    d_expert = wd.shape[1]
    grid_spec = pltpu.PrefetchScalarGridSpec(
        num_scalar_prefetch=2,
        grid=(n_blk,),
        in_specs=[
            pl.BlockSpec((1, 1, tb), lambda i, be, nv: (i, 0, 0), memory_space=pltpu.SMEM),
            pl.BlockSpec((1, 1, tb), lambda i, be, nv: (jnp.minimum(i + 1, n_blk - 1), 0, 0),
                         memory_space=pltpu.SMEM),
            pl.BlockSpec(memory_space=pl.ANY),
            pl.BlockSpec((None, d, 2 * d_expert), lambda i, be, nv: (be[i], 0, 0)),
            pl.BlockSpec((None, d_expert, d), lambda i, be, nv: (be[i], 0, 0)),
        ],
        out_specs=pl.BlockSpec((tb, d), lambda i, be, nv: (i, 0)),
        scratch_shapes=[
            pltpu.VMEM((2, tb, d), F32),
            pltpu.SemaphoreType.DMA((2,)),
        ],
    )
    return pl.pallas_call(
        functools.partial(_experts_kernel, d_expert=d_expert),
        grid_spec=grid_spec,
        out_shape=jax.ShapeDtypeStruct((n_blk * tb, d), F32),
        compiler_params=pltpu.CompilerParams(
            dimension_semantics=("arbitrary",), vmem_limit_bytes=VMEM_LIMIT),
        name="experts",
    )(blk_e, nvalid, tok_blocks, tok_blocks, h2, wgu, wd)


def _combine_kernel(pos_ref, pos_next_ref, x1_ref, rt_ref, gf_ref, y_hbm, o_ref, ybuf, sem):
    i = pl.program_id(0)
    n_steps = pl.num_programs(0)
    tm = x1_ref.shape[0]
    slot = i % 2

    @pl.when(i == 0)
    def _():
        _start_row_gather(y_hbm, pos_ref, ybuf.at[0], sem.at[0], 2 * tm)

    @pl.when(i + 1 < n_steps)
    def _():
        _start_row_gather(y_hbm, pos_next_ref, ybuf.at[1 - slot], sem.at[1 - slot], 2 * tm)

    _wait_row_gather(y_hbm, ybuf.at[slot], sem.at[slot], 2 * tm)
    rt = rt_ref[...]
    moe = rt[:, 0:1] * ybuf[slot, 0:tm, :] + rt[:, 1:2] * ybuf[slot, tm:2 * tm, :]
    x = x1_ref[...] + moe
    ms = jnp.mean(x * x, axis=-1, keepdims=True)
    o_ref[...] = x * lax.rsqrt(ms + RMS_EPS) * gf_ref[...]


def _combine(x1, rt, gf, y, pos_blocks, tm):
    n, d = x1.shape
    n_steps = n // tm
    return pl.pallas_call(
        _combine_kernel,
        grid=(n_steps,),
        in_specs=[
            pl.BlockSpec((1, 1, 2 * tm), lambda i: (i, 0, 0), memory_space=pltpu.SMEM),
            pl.BlockSpec((1, 1, 2 * tm), lambda i: (jnp.minimum(i + 1, n_steps - 1), 0, 0),
                         memory_space=pltpu.SMEM),
            pl.BlockSpec((tm, d), lambda i: (i, 0)),
            pl.BlockSpec((tm, LANES), lambda i: (i, 0)),
            pl.BlockSpec((1, d), lambda i: (0, 0)),
            pl.BlockSpec(memory_space=pl.ANY),
        ],
        out_specs=pl.BlockSpec((tm, d), lambda i: (i, 0)),
        out_shape=jax.ShapeDtypeStruct((n, d), F32),
        scratch_shapes=[
            pltpu.VMEM((2, 2 * tm, d), F32),
            pltpu.SemaphoreType.DMA((2,)),
        ],
        compiler_params=pltpu.CompilerParams(
            dimension_semantics=("arbitrary",), vmem_limit_bytes=VMEM_LIMIT),
        name="combine",
    )(pos_blocks, pos_blocks, x1, rt, gf.reshape(1, d), y)


def _dispatch_tables(expert_id, n_experts, tb, tm):
    n = expert_id.shape[0]
    a = n * EXPERT_TOPK
    e_flat = expert_id.reshape(a)
    tok_flat = jnp.arange(a, dtype=jnp.int32) // EXPERT_TOPK
    order = jnp.argsort(e_flat)
    e_sorted = e_flat[order]
    counts = jnp.bincount(e_flat, length=n_experts).astype(jnp.int32)
    padded = ((counts + tb - 1) // tb) * tb
    start = jnp.cumsum(counts) - counts
    pend = jnp.cumsum(padded)
    pstart = pend - padded
    dest = pstart[e_sorted] + (jnp.arange(a, dtype=jnp.int32) - start[e_sorted])
    n_blk = (a + n_experts * tb) // tb
    buf_tok = jnp.zeros((n_blk * tb,), jnp.int32).at[dest].set(tok_flat[order])
    pos = jnp.zeros((a,), jnp.int32).at[order].set(dest)
    blk_start = jnp.arange(n_blk, dtype=jnp.int32) * tb
    blk_e = jnp.clip(jnp.sum(pend[None, :] <= blk_start[:, None], axis=1), 0,
                     n_experts - 1).astype(jnp.int32)
    nvalid = (pend[-1] // tb).astype(jnp.int32).reshape(1)
    tok_blocks = buf_tok.reshape(n_blk, 1, tb)
    pos_blocks = pos.reshape(n // tm, tm, EXPERT_TOPK).transpose(0, 2, 1).reshape(
        n // tm, 1, EXPERT_TOPK * tm)
    return tok_blocks, blk_e, nvalid, pos_blocks


def _pick_tile(n, pref):
    t = pref
    while n % t:
        t //= 2
    return t


def kernel(x, norm1_g, w_in, b_gates, w_out_moba, w_out_dil, w_o, norm2_g, w_group, b_group,
           w_expert, b_expert, w_gate, w_up, w_down, norm_f_g):
    bsz, seq, d = x.shape
    n = bsz * seq
    assert w_in.shape[0] == 1, "one layer: the final RMSNorm is fused into the combine stage"
    width = w_out_moba.shape[1]
    assert w_out_dil.shape[1] == width and width % HEAD_DIM == 0
    n_heads = width // HEAD_DIM
    n_groups = w_group.shape[-1]
    n_experts = w_expert.shape[-1]
    per_group = n_experts // n_groups
    assert n_groups + n_experts <= LANES
    slopes_m, slopes_d = _alibi_slopes(n_heads, n_heads)
    tm_in = _pick_tile(n, 512)
    tm_out = _pick_tile(n, 256)
    tb = 256
    tm_c = _pick_tile(n, 256)
    l = 0

    x2 = x.reshape(n, d)
    projm, projd, gates = _inproj(x2, norm1_g[l], w_in[l].astype(BF16), b_gates[l], width, tm_in)
    o_m = _moba(projm, slopes_m, bsz, seq, n_heads)
    o_d = _dilated(projd, slopes_d, bsz, seq, n_heads)

    w_r = jnp.concatenate([w_group[l], w_expert[l]], axis=1)
    w_r = jnp.pad(w_r, ((0, 0), (0, LANES - w_r.shape[1])))
    wr_hi = w_r.astype(BF16)
    wr_lo = (w_r - wr_hi.astype(F32)).astype(BF16)
    b_r = jnp.pad(jnp.concatenate([b_group[l], b_expert[l]]),
                  (0, LANES - n_groups - n_experts)).reshape(1, LANES)
    x1, h2, rt = _outproj(o_m, o_d, gates, x2, w_out_moba[l].astype(BF16),
                          w_out_dil[l].astype(BF16), w_o[l].astype(BF16), norm2_g[l],
                          wr_hi, wr_lo, b_r, n_groups, per_group, tm_out)

    expert_id = rt[:, 2:2 + EXPERT_TOPK].astype(jnp.int32)
    tok_blocks, blk_e, nvalid, pos_blocks = _dispatch_tables(expert_id, n_experts, tb, tm_c)
    wgu = jnp.concatenate([w_gate[l], w_up[l]], axis=-1).astype(BF16)
    y = _experts(h2, tok_blocks, blk_e, nvalid, wgu, w_down[l].astype(BF16), tb)
    out = _combine(x1, rt, norm_f_g, y, pos_blocks, tm_c)
    return out.reshape(bsz, seq, d)
```

```python
import functools

import numpy as np
import jax
import jax.numpy as jnp
from jax import lax
from jax.experimental import pallas as pl
from jax.experimental.pallas import tpu as pltpu

HEAD_DIM = 128
MOBA_BLOCK = 256
MOBA_TOPK = 3
DIL_PATTERNS = ((128, 1), (512, 4), (2048, 16))
DIL_STEPS = 128
DIL_CHUNK = 2048
DIL_BATCH = 8
EXPERT_TOPK = 2
RMS_EPS = 1e-6
NEG = -1e30
LOG2E = 1.4426950408889634
LANES = 128
VMEM_LIMIT = 56 * 1024 * 1024

F32 = jnp.float32
BF16 = jnp.bfloat16


def _alibi_slopes(n_moba, n_dil):
    n = n_moba + n_dil
    s = 2.0 ** (-8.0 * np.arange(1, n + 1) / n)
    return jnp.asarray(s[0::2], F32), jnp.asarray(s[1::2], F32)


def _inproj_kernel(x_ref, g_ref, w_ref, b_ref, om_ref, od_ref, og_ref, h_scr, *, qscale):
    j = pl.program_id(1)

    @pl.when(j == 0)
    def _():
        x = x_ref[...]
        ms = jnp.mean(x * x, axis=-1, keepdims=True)
        h_scr[...] = (x * lax.rsqrt(ms + RMS_EPS) * g_ref[...]).astype(BF16)

    acc = jnp.dot(h_scr[...], w_ref[...], preferred_element_type=F32)

    @pl.when(j == 0)
    def _():
        om_ref[...] = (acc * qscale).astype(BF16)

    @pl.when((j > 0) & (j < 3))
    def _():
        om_ref[...] = acc.astype(BF16)

    @pl.when(j == 3)
    def _():
        od_ref[...] = acc * qscale

    @pl.when((j > 3) & (j < 6))
    def _():
        od_ref[...] = acc

    @pl.when(j >= 6)
    def _():
        og_ref[...] = jax.nn.sigmoid(acc + b_ref[...]).astype(BF16)


def _inproj(x2, g1, w_in, b_gates, width, tm):
    n, d = x2.shape
    cols = w_in.shape[1]
    tn = width
    assert cols == 6 * width + 2 * d and d % tn == 0 and n % tm == 0
    n_gate = (2 * d) // tn
    qscale = HEAD_DIM ** -0.5 * LOG2E
    return pl.pallas_call(
        functools.partial(_inproj_kernel, qscale=qscale),
        grid=(n // tm, 6 + n_gate),
        in_specs=[
            pl.BlockSpec((tm, d), lambda i, j: (i, 0)),
            pl.BlockSpec((1, d), lambda i, j: (0, 0)),
            pl.BlockSpec((d, tn), lambda i, j: (0, j)),
            pl.BlockSpec((1, tn), lambda i, j: (0, jnp.clip(j - 6, 0, n_gate - 1))),
        ],
        out_specs=[
            pl.BlockSpec((tm, tn), lambda i, j: (i, jnp.minimum(j, 2))),
            pl.BlockSpec((tm, tn), lambda i, j: (i, jnp.clip(j - 3, 0, 2))),
            pl.BlockSpec((tm, tn), lambda i, j: (i, jnp.clip(j - 6, 0, n_gate - 1))),
        ],
        out_shape=[
            jax.ShapeDtypeStruct((n, 3 * width), BF16),
            jax.ShapeDtypeStruct((n, 3 * width), F32),
            jax.ShapeDtypeStruct((n, 2 * d), BF16),
        ],
        scratch_shapes=[pltpu.VMEM((tm, d), BF16)],
        compiler_params=pltpu.CompilerParams(
            dimension_semantics=("arbitrary", "arbitrary"), vmem_limit_bytes=VMEM_LIMIT),
        name="inproj",
    )(x2, g1.reshape(1, d), w_in, b_gates.reshape(1, 2 * d))


def _nt_dot(a, b):
    return lax.dot_general(a, b, (((1,), (1,)), ((), ())), preferred_element_type=F32)


def _dot(a, b):
    return jnp.dot(a, b, preferred_element_type=F32)


MOBA_TILE = 2 * MOBA_BLOCK
ALIBI_SPLIT = 3
VT_ROWS = HEAD_DIM + 16


def _moba_kernel(slopes_ref, q_ref, k_ref, v_ref, o_ref, kaug, vt, kmh, kml, causal, sbuf, pbuf,
                 *, nb, nbp, hpg):
    blk = MOBA_BLOCK
    hd = HEAD_DIM
    tq = tk = MOBA_TILE
    hg = pl.program_id(1)
    qp = pl.program_id(2)
    heads = range(hpg)
    slope2 = [slopes_ref[hg * hpg + g] * LOG2E for g in heads]
    cols = [slice(g * hd, (g + 1) * hd) for g in heads]
    s_len = k_ref.shape[0]
    c_hi0 = nbp + ALIBI_SPLIT

    @pl.when(qp == 0)
    def _init():
        pos = lax.broadcasted_iota(jnp.int32, (s_len, LANES), 0)
        lane = lax.broadcasted_iota(jnp.int32, (s_len, LANES), 1)
        off = pos % tk
        aug = jnp.where(lane < nbp, jnp.where(pos // blk == lane, 1.0, 0.0),
                        jnp.where(lane < c_hi0, (off % blk).astype(F32),
                                  jnp.where(lane < c_hi0 + ALIBI_SPLIT,
                                            (off - off % blk).astype(F32), 0.0))).astype(BF16)
        row = lax.broadcasted_iota(jnp.int32, (VT_ROWS - hd, s_len), 0)
        ones_row = jnp.where(row == 0, 1.0, 0.0).astype(BF16)
        for g in heads:
            kaug[g, :, 0:hd] = k_ref[:, cols[g]]
            kaug[g, :, hd:hd + LANES] = aug

            def xpose(n, carry, g=g):
                r0 = pl.multiple_of(n * blk, blk)
                vt[g, 0:hd, pl.ds(r0, blk)] = (
                    v_ref[pl.ds(r0, blk), cols[g]].astype(F32).T.astype(BF16))
                return carry

            lax.fori_loop(0, nb, xpose, 0)
            vt[g, hd:VT_ROWS, :] = ones_row
            km = jnp.sum(k_ref[:, cols[g]].astype(F32).reshape(nb, blk, hd), axis=1) * (1.0 / blk)
            if nb < nbp:
                km = jnp.concatenate([km, jnp.zeros((nbp - nb, hd), F32)], axis=0)
            hi = km.astype(BF16)
            kmh[g] = hi
            kml[g] = (km - hi.astype(F32)).astype(BF16)
        cc = lax.broadcasted_iota(jnp.int32, (tk, tq), 0)
        rr = lax.broadcasted_iota(jnp.int32, (tk, tq), 1)
        causal[...] = jnp.where(cc > rr, NEG, 0.0)

    lane_q = lax.broadcasted_iota(jnp.int32, (1, tq), 1)
    qblk = qp * (tq // blk) + lane_q // blk
    t_q = qp * tq + lane_q
    bidx = lax.broadcasted_iota(jnp.int32, (nbp, tq), 0)
    ri = lax.broadcasted_iota(jnp.int32, (16, tq), 0)
    tail = jnp.zeros((LANES - nbp - 16, tq), BF16)

    def augmented_queries(g):
        qt = q_ref[:, cols[g]].astype(F32).T.astype(BF16)
        gate = _dot(kmh[g], qt) + _dot(kml[g], qt)
        gt = jnp.where(bidx < qblk, gate, -jnp.inf)
        selneg = jnp.full((nbp, tq), NEG, F32)
        for _ in range(min(MOBA_TOPK, nb)):
            mx = jnp.max(gt, axis=0, keepdims=True)
            idx = jnp.min(jnp.where(gt == mx, bidx, nbp), axis=0, keepdims=True)
            hit = bidx == idx
            selneg = jnp.where(hit & (mx > -jnp.inf), 0.0, selneg)
            gt = jnp.where(hit, -jnp.inf, gt)
        sv = jnp.full((16, tq), slope2[g], F32)
        pieces = []
        for _ in range(ALIBI_SPLIT):
            pc = sv.astype(BF16).astype(F32)
            pieces.append(pc)
            sv = sv - pc
        srows = jnp.zeros((16, tq), F32)
        for j in range(2 * ALIBI_SPLIT):
            srows = jnp.where(ri == j, pieces[j % ALIBI_SPLIT], srows)

        def augment(sel):
            return jnp.concatenate([qt, sel.astype(BF16), srows.astype(BF16), tail], axis=0)

        return augment(selneg), augment(jnp.where(bidx == qblk, 0.0, selneg))

    qaugs = [augmented_queries(g) for g in heads]

    def scores(g, k0, qa):
        return _dot(kaug[g, pl.ds(k0, tk), :], qa)

    def softmax(g, k0, raw, m):
        shift = slope2[g] * (k0 - t_q).astype(F32)
        m_new = jnp.maximum(m, jnp.max(raw, axis=0, keepdims=True) + shift)
        return m_new, jnp.exp2(raw - (m_new - shift)), jnp.exp2(m - m_new)

    def values(g, k0):
        return _dot(vt[g, :, pl.ds(k0, tk)], pbuf[g])

    for g in heads:
        sbuf[g] = scores(g, pl.multiple_of(qp * tk, tk), qaugs[g][1]) + causal[...]
        pbuf[g] = jnp.zeros((tk, tq), BF16)

    def step(j, st):
        k_cur = pl.multiple_of((qp - j) * tk, tk)
        k_prev = pl.multiple_of(jnp.minimum(k_cur + tk, qp * tk), tk)
        raws = [scores(g, k_cur - tk, qaugs[g][0]) for g in heads]
        pvs = [values(g, k_prev) for g in heads]
        out = []
        for g in heads:
            m, alpha_prev, acc = st[g]
            m_new, p, alpha = softmax(g, k_cur, sbuf[g], m)
            acc = alpha_prev * acc + pvs[g]
            pbuf[g] = p.astype(BF16)
            sbuf[g] = raws[g]
            out.append((m_new, alpha, acc))
        return tuple(out)

    init = (jnp.full((1, tq), -jnp.inf, F32), jnp.ones((1, tq), F32),
            jnp.zeros((VT_ROWS, tq), F32))
    fin = lax.fori_loop(0, qp, step, tuple(init for _ in heads))
    k_last = pl.multiple_of(jnp.minimum(tk, qp * tk), tk)
    for g in heads:
        m, alpha_prev, acc = fin[g]
        acc = alpha_prev * acc + values(g, k_last)
        _, p, alpha = softmax(g, 0, sbuf[g], m)
        pbuf[g] = p.astype(BF16)
        acc = alpha * acc + values(g, 0)
        o_ref[:, cols[g]] = (acc[0:hd, :] / acc[hd:hd + 1, :]).T.astype(o_ref.dtype)


def _moba(projm, slopes, bsz, seq, n_heads):
    width = n_heads * HEAD_DIM
    nb = seq // MOBA_BLOCK
    nbp = -(-nb // 16) * 16
    tile = MOBA_TILE
    hpg = 2 if n_heads % 2 == 0 else 1
    ng = n_heads // hpg
    gw = hpg * HEAD_DIM
    assert seq % tile == 0 and nbp + 16 <= LANES
    pm = projm.reshape(bsz, seq, 3 * width)
    grid_spec = pltpu.PrefetchScalarGridSpec(
        num_scalar_prefetch=1,
        grid=(bsz, ng, seq // tile),
        in_specs=[
            pl.BlockSpec((None, tile, gw), lambda b, h, i, s: (b, i, h)),
            pl.BlockSpec((None, seq, gw), lambda b, h, i, s: (b, 0, ng + h)),
            pl.BlockSpec((None, seq, gw), lambda b, h, i, s: (b, 0, 2 * ng + h)),
        ],
        out_specs=pl.BlockSpec((None, tile, gw), lambda b, h, i, s: (b, i, h)),
        scratch_shapes=[
            pltpu.VMEM((hpg, seq, HEAD_DIM + LANES), BF16),
            pltpu.VMEM((hpg, VT_ROWS, seq), BF16),
            pltpu.VMEM((hpg, nbp, HEAD_DIM), BF16),
            pltpu.VMEM((hpg, nbp, HEAD_DIM), BF16),
            pltpu.VMEM((tile, tile), F32),
            pltpu.VMEM((hpg, tile, tile), F32),
            pltpu.VMEM((hpg, tile, tile), BF16),
        ],
    )
    out = pl.pallas_call(
        functools.partial(_moba_kernel, nb=nb, nbp=nbp, hpg=hpg),
        grid_spec=grid_spec,
        out_shape=jax.ShapeDtypeStruct((bsz, seq, width), BF16),
        compiler_params=pltpu.CompilerParams(
            dimension_semantics=("arbitrary", "arbitrary", "arbitrary"),
            vmem_limit_bytes=VMEM_LIMIT),
        name="moba",
    )(slopes, pm, pm, pm)
    return out.reshape(bsz * seq, width)


def _dil_kernel(slopes_ref, q_ref, kc_ref, kp_ref, vc_ref, vp_ref, o_ref, bias, osc, lsc):
    w = DIL_STEPS
    chunk = q_ref.shape[0]
    h = pl.program_id(1)
    c = pl.program_id(2)
    slope2 = slopes_ref[h] * LOG2E

    @pl.when(c == 0)
    def _init():
        i = lax.broadcasted_iota(jnp.int32, (w, 2 * w), 0)
        j = lax.broadcasted_iota(jnp.int32, (w, 2 * w), 1)
        diff = i + w - j
        ok = (diff >= 0) & (diff <= w)
        for p, (_, d) in enumerate(DIL_PATTERNS):
            bias[p] = jnp.where(ok, diff.astype(F32) * (-slope2 * d), NEG)

    jj = lax.broadcasted_iota(jnp.int32, (w, 2 * w), 1)
    first_pen = jnp.where(jj < w, jnp.where(c > 0, 0.0, NEG), 0.0)

    def rows(start, d):
        return pl.ds(start, w) if d == 1 else pl.ds(start, w, stride=d)

    def tiles(p, d, qstarts, kprev_ref, vprev_ref, pstarts, from_prev_chunk):
        n = len(qstarts)
        ss = []
        for qstart, pstart in zip(qstarts, pstarts):
            q = q_ref[rows(qstart, d), :].astype(BF16)
            k2 = jnp.concatenate([kprev_ref[rows(pstart, d), :], kc_ref[rows(qstart, d), :]],
                                 axis=0).astype(BF16)
            ss.append(_nt_dot(q, k2))
        prs, ms, ls = [], [], []
        for s in ss:
            s = s + bias[p]
            if from_prev_chunk:
                s = s + first_pen
            m = jnp.max(s, axis=-1, keepdims=True)
            pr = jnp.exp2(s - m)
            prs.append(pr.astype(BF16))
            ms.append(m)
            ls.append(jnp.sum(pr, axis=-1, keepdims=True))
        for i in range(n):
            v2 = jnp.concatenate([vprev_ref[rows(pstarts[i], d), :],
                                  vc_ref[rows(qstarts[i], d), :]], axis=0).astype(BF16)
            o = jnp.dot(prs[i], v2, preferred_element_type=F32) / ls[i]
            osc[p, rows(qstarts[i], d), :] = o
            lsc[p, rows(qstarts[i], d), :] = jnp.broadcast_to(ms[i] + jnp.log2(ls[i]), (w, LANES))

    def batch_size(n):
        return max(g for g in range(1, DIL_BATCH + 1) if n % g == 0)

    for p, (_, d) in enumerate(DIL_PATTERNS):
        span = w * d
        n_sb = chunk // span

        g1 = batch_size(d)

        def first(t, carry, p=p, d=d, span=span, g1=g1):
            rs = [t * g1 + i for i in range(g1)]
            tiles(p, d, rs, kp_ref, vp_ref, [chunk - span + r for r in rs], True)
            return carry

        lax.fori_loop(0, d // g1, first, 0)

        n_rest = (n_sb - 1) * d
        if n_rest:
            g2 = batch_size(n_rest)

            def rest(t, carry, p=p, d=d, span=span, g2=g2):
                idx = [t * g2 + i for i in range(g2)]
                qstarts = [(1 + x // d) * span + x % d for x in idx]
                if d == 1:
                    qstarts = [pl.multiple_of(x, w) for x in qstarts]
                tiles(p, d, qstarts, kc_ref, vc_ref, [x - span for x in qstarts], False)
                return carry

            lax.fori_loop(0, n_rest // g2, rest, 0)

    rb = 256

    def merge(t, carry):
        r0 = pl.multiple_of(t * rb, rb)
        ls = [lsc[p, pl.ds(r0, rb), :] for p in range(len(DIL_PATTERNS))]
        mx = functools.reduce(jnp.maximum, ls)
        ws = [jnp.exp2(x - mx) for x in ls]
        num = ws[0] * osc[0, pl.ds(r0, rb), :]
        den = ws[0]
        for p in range(1, len(DIL_PATTERNS)):
            num = num + ws[p] * osc[p, pl.ds(r0, rb), :]
            den = den + ws[p]
        o_ref[pl.ds(r0, rb), :] = (num / den).astype(o_ref.dtype)
        return carry

    lax.fori_loop(0, chunk // rb, merge, 0)


def _dilated(projd, slopes, bsz, seq, n_heads):
    width = n_heads * HEAD_DIM
    chunk = DIL_CHUNK
    assert seq % chunk == 0
    assert all(win // d == DIL_STEPS and chunk % win == 0 for win, d in DIL_PATTERNS)
    pd_ = projd.reshape(bsz, seq, 3 * width)
    blk = (None, chunk, HEAD_DIM)
    npat = len(DIL_PATTERNS)
    grid_spec = pltpu.PrefetchScalarGridSpec(
        num_scalar_prefetch=1,
        grid=(bsz, n_heads, seq // chunk),
        in_specs=[
            pl.BlockSpec(blk, lambda b, h, c, s: (b, c, h)),
            pl.BlockSpec(blk, lambda b, h, c, s: (b, c, n_heads + h)),
            pl.BlockSpec(blk, lambda b, h, c, s: (b, jnp.maximum(c - 1, 0), n_heads + h)),
            pl.BlockSpec(blk, lambda b, h, c, s: (b, c, 2 * n_heads + h)),
            pl.BlockSpec(blk, lambda b, h, c, s: (b, jnp.maximum(c - 1, 0), 2 * n_heads + h)),
        ],
        out_specs=pl.BlockSpec(blk, lambda b, h, c, s: (b, c, h)),
        scratch_shapes=[
            pltpu.VMEM((npat, DIL_STEPS, 2 * DIL_STEPS), F32),
            pltpu.VMEM((npat, chunk, HEAD_DIM), F32),
            pltpu.VMEM((npat, chunk, LANES), F32),
        ],
    )
    out = pl.pallas_call(
        _dil_kernel,
        grid_spec=grid_spec,
        out_shape=jax.ShapeDtypeStruct((bsz, seq, width), BF16),
        compiler_params=pltpu.CompilerParams(
            dimension_semantics=("arbitrary", "arbitrary", "arbitrary"),
            vmem_limit_bytes=VMEM_LIMIT),
        name="dilated",
    )(slopes, pd_, pd_, pd_, pd_, pd_)
    return out.reshape(bsz * seq, width)


def _outproj_kernel(am_ref, ad_ref, gm_ref, gd_ref, x_ref, wm_ref, wd_ref, wo_ref, g2_ref,
                    wrh_ref, wrl_ref, br_ref, x1_ref, h2_ref, rt_ref, *, n_groups, per_group):
    y_m = jnp.dot(am_ref[...], wm_ref[...], preferred_element_type=F32)
    y_d = jnp.dot(ad_ref[...], wd_ref[...], preferred_element_type=F32)
    z = (gm_ref[...].astype(F32) * y_m + gd_ref[...].astype(F32) * y_d).astype(BF16)
    x1 = x_ref[...] + jnp.dot(z, wo_ref[...], preferred_element_type=F32)
    x1_ref[...] = x1
    ms = jnp.mean(x1 * x1, axis=-1, keepdims=True)
    h2 = x1 * lax.rsqrt(ms + RMS_EPS) * g2_ref[...]
    h2_ref[...] = h2

    hi = h2.astype(BF16)
    lo = (h2 - hi.astype(F32)).astype(BF16)
    logits = (jnp.dot(hi, wrh_ref[...], preferred_element_type=F32)
              + (jnp.dot(hi, wrl_ref[...], preferred_element_type=F32)
                 + jnp.dot(lo, wrh_ref[...], preferred_element_type=F32))
              + br_ref[...])

    lane = lax.broadcasted_iota(jnp.int32, logits.shape, 1)
    big = jnp.int32(1 << 20)
    isg = lane < n_groups
    gl = jnp.where(isg, logits, -jnp.inf)
    gex = jnp.exp(gl - jnp.max(gl, axis=-1, keepdims=True))
    gprob = gex / jnp.sum(gex, axis=-1, keepdims=True)
    gtp = jnp.max(gprob, axis=-1, keepdims=True)
    gtop = jnp.min(jnp.where((gprob == gtp) & isg, lane, big), axis=-1, keepdims=True)
    e_lo = n_groups + gtop * per_group
    ise = (lane >= e_lo) & (lane < e_lo + per_group)
    el = jnp.where(ise, logits, -jnp.inf)
    eex = jnp.exp(el - jnp.max(el, axis=-1, keepdims=True))
    eprob = jnp.where(ise, eex / jnp.sum(eex, axis=-1, keepdims=True), -1.0)
    p1 = jnp.max(eprob, axis=-1, keepdims=True)
    i1 = jnp.min(jnp.where(eprob == p1, lane, big), axis=-1, keepdims=True)
    eprob2 = jnp.where(lane == i1, -1.0, eprob)
    p2 = jnp.max(eprob2, axis=-1, keepdims=True)
    i2 = jnp.min(jnp.where(eprob2 == p2, lane, big), axis=-1, keepdims=True)
    den = p1 + p2
    w1 = gtp * p1 / den
    w2 = gtp * p2 / den
    e1 = (i1 - n_groups).astype(F32)
    e2 = (i2 - n_groups).astype(F32)
    rt_ref[...] = jnp.where(lane == 0, w1, jnp.where(lane == 1, w2, jnp.where(
        lane == 2, e1, jnp.where(lane == 3, e2, 0.0))))


def _outproj(a_m, a_d, gates, x2, w_m, w_d, w_o, g2, wr_hi, wr_lo, b_r, n_groups, per_group, tm):
    n, d = x2.shape
    width = a_m.shape[1]
    assert n % tm == 0

    def const(shape):
        return pl.BlockSpec(shape, lambda i: (0, 0))

    return pl.pallas_call(
        functools.partial(_outproj_kernel, n_groups=n_groups, per_group=per_group),
        grid=(n // tm,),
        in_specs=[
            pl.BlockSpec((tm, width), lambda i: (i, 0)),
            pl.BlockSpec((tm, width), lambda i: (i, 0)),
            pl.BlockSpec((tm, d), lambda i: (i, 0)),
            pl.BlockSpec((tm, d), lambda i: (i, 1)),
            pl.BlockSpec((tm, d), lambda i: (i, 0)),
            const((width, d)), const((width, d)), const((d, d)), const((1, d)),
            const((d, LANES)), const((d, LANES)), const((1, LANES)),
        ],
        out_specs=[
            pl.BlockSpec((tm, d), lambda i: (i, 0)),
            pl.BlockSpec((tm, d), lambda i: (i, 0)),
            pl.BlockSpec((tm, LANES), lambda i: (i, 0)),
        ],
        out_shape=[
            jax.ShapeDtypeStruct((n, d), F32),
            jax.ShapeDtypeStruct((n, d), F32),
            jax.ShapeDtypeStruct((n, LANES), F32),
        ],
        compiler_params=pltpu.CompilerParams(
            dimension_semantics=("arbitrary",), vmem_limit_bytes=VMEM_LIMIT),
        name="outproj",
    )(a_m, a_d, gates, gates, x2, w_m, w_d, w_o, g2.reshape(1, d), wr_hi, wr_lo, b_r)


DMA_PRIORITIES = 2


def _start_row_gather(src_hbm, idx_ref, dst, sem, n_rows):
    def issue(t, carry):
        for k in range(DMA_PRIORITIES):
            r = t * DMA_PRIORITIES + k
            pltpu.make_async_copy(src_hbm.at[pl.ds(idx_ref[0, 0, r], 1), :],
                                  dst.at[pl.ds(r, 1), :], sem).start(priority=k)
        return carry

    assert n_rows % DMA_PRIORITIES == 0
    lax.fori_loop(0, n_rows // DMA_PRIORITIES, issue, 0)


def _wait_row_gather(src_hbm, dst, sem, n_rows):
    pltpu.make_async_copy(src_hbm.at[pl.ds(0, n_rows), :], dst, sem).wait()


def _experts_kernel(blk_e_ref, nvalid_ref, tok_ref, tok_next_ref, h_hbm, wgu_ref, wd_ref, y_ref,
                    xbuf, sem, *, d_expert):
    del blk_e_ref
    i = pl.program_id(0)
    nvalid = nvalid_ref[0]
    tb = xbuf.shape[1]
    slot = i % 2

    @pl.when((i == 0) & (i < nvalid))
    def _():
        _start_row_gather(h_hbm, tok_ref, xbuf.at[0], sem.at[0], tb)

    @pl.when(i + 1 < nvalid)
    def _():
        _start_row_gather(h_hbm, tok_next_ref, xbuf.at[1 - slot], sem.at[1 - slot], tb)

    @pl.when(i < nvalid)
    def _():
        _wait_row_gather(h_hbm, xbuf.at[slot], sem.at[slot], tb)
        x = xbuf[slot].astype(BF16)
        gu = jnp.dot(x, wgu_ref[...], preferred_element_type=F32)
        a = (jax.nn.silu(gu[:, :d_expert]) * gu[:, d_expert:]).astype(BF16)
        y_ref[...] = jnp.dot(a, wd_ref[...], preferred_element_type=F32)

    @pl.when(i >= nvalid)
    def _():
        y_ref[...] = jnp.zeros_like(y_ref)


def _experts(h2, tok_blocks, blk_e, nvalid, wgu, wd, tb):
    n, d = h2.shape
    n_blk = tok_blocks.shape[0]
    d_expert = wd.shape[1]
    grid_spec = pltpu.PrefetchScalarGridSpec(
        num_scalar_prefetch=2,
        grid=(n_blk,),
        in_specs=[
            pl.BlockSpec((1, 1, tb), lambda i, be, nv: (i, 0, 0), memory_space=pltpu.SMEM),
            pl.BlockSpec((1, 1, tb), lambda i, be, nv: (jnp.minimum(i + 1, n_blk - 1), 0, 0),
                         memory_space=pltpu.SMEM),
            pl.BlockSpec(memory_space=pl.ANY),
            pl.BlockSpec((None, d, 2 * d_expert), lambda i, be, nv: (be[i], 0, 0)),
            pl.BlockSpec((None, d_expert, d), lambda i, be, nv: (be[i], 0, 0)),
        ],
        out_specs=pl.BlockSpec((tb, d), lambda i, be, nv: (i, 0)),
        scratch_shapes=[
            pltpu.VMEM((2, tb, d), F32),
            pltpu.SemaphoreType.DMA((2,)),
        ],
    )
    return pl.pallas_call(
        functools.partial(_experts_kernel, d_expert=d_expert),
        grid_spec=grid_spec,
        out_shape=jax.ShapeDtypeStruct((n_blk * tb, d), F32),
        compiler_params=pltpu.CompilerParams(
            dimension_semantics=("arbitrary",), vmem_limit_bytes=VMEM_LIMIT),
        name="experts",
    )(blk_e, nvalid, tok_blocks, tok_blocks, h2, wgu, wd)


def _combine_kernel(pos_ref, pos_next_ref, x1_ref, rt_ref, gf_ref, y_hbm, o_ref, ybuf, sem):
    i = pl.program_id(0)
    n_steps = pl.num_programs(0)
    tm = x1_ref.shape[0]
    slot = i % 2

    @pl.when(i == 0)
    def _():
        _start_row_gather(y_hbm, pos_ref, ybuf.at[0], sem.at[0], 2 * tm)

    @pl.when(i + 1 < n_steps)
    def _():
        _start_row_gather(y_hbm, pos_next_ref, ybuf.at[1 - slot], sem.at[1 - slot], 2 * tm)

    _wait_row_gather(y_hbm, ybuf.at[slot], sem.at[slot], 2 * tm)
    rt = rt_ref[...]
    moe = rt[:, 0:1] * ybuf[slot, 0:tm, :] + rt[:, 1:2] * ybuf[slot, tm:2 * tm, :]
    x = x1_ref[...] + moe
    ms = jnp.mean(x * x, axis=-1, keepdims=True)
    o_ref[...] = x * lax.rsqrt(ms + RMS_EPS) * gf_ref[...]


def _combine(x1, rt, gf, y, pos_blocks, tm):
    n, d = x1.shape
    n_steps = n // tm
    return pl.pallas_call(
        _combine_kernel,
        grid=(n_steps,),
        in_specs=[
            pl.BlockSpec((1, 1, 2 * tm), lambda i: (i, 0, 0), memory_space=pltpu.SMEM),
            pl.BlockSpec((1, 1, 2 * tm), lambda i: (jnp.minimum(i + 1, n_steps - 1), 0, 0),
                         memory_space=pltpu.SMEM),
            pl.BlockSpec((tm, d), lambda i: (i, 0)),
            pl.BlockSpec((tm, LANES), lambda i: (i, 0)),
            pl.BlockSpec((1, d), lambda i: (0, 0)),
            pl.BlockSpec(memory_space=pl.ANY),
        ],
        out_specs=pl.BlockSpec((tm, d), lambda i: (i, 0)),
        out_shape=jax.ShapeDtypeStruct((n, d), F32),
        scratch_shapes=[
            pltpu.VMEM((2, 2 * tm, d), F32),
            pltpu.SemaphoreType.DMA((2,)),
        ],
        compiler_params=pltpu.CompilerParams(
            dimension_semantics=("arbitrary",), vmem_limit_bytes=VMEM_LIMIT),
        name="combine",
    )(pos_blocks, pos_blocks, x1, rt, gf.reshape(1, d), y)


def _dispatch_tables(expert_id, n_experts, tb, tm):
    n = expert_id.shape[0]
    a = n * EXPERT_TOPK
    e_flat = expert_id.reshape(a)
    tok_flat = jnp.arange(a, dtype=jnp.int32) // EXPERT_TOPK
    order = jnp.argsort(e_flat)
    e_sorted = e_flat[order]
    counts = jnp.bincount(e_flat, length=n_experts).astype(jnp.int32)
    padded = ((counts + tb - 1) // tb) * tb
    start = jnp.cumsum(counts) - counts
    pend = jnp.cumsum(padded)
    pstart = pend - padded
    dest = pstart[e_sorted] + (jnp.arange(a, dtype=jnp.int32) - start[e_sorted])
    n_blk = (a + n_experts * tb) // tb
    blk_start = jnp.arange(n_blk, dtype=jnp.int32) * tb
    blk_e = jnp.clip(jnp.sum(pend[None, :] <= blk_start[:, None], axis=1), 0,
                     n_experts - 1).astype(jnp.int32)
    nvalid = (pend[-1] // tb).astype(jnp.int32).reshape(1)
    row = jnp.arange(n_blk * tb, dtype=jnp.int32)
    e_row = jnp.repeat(blk_e, tb)
    i_in = row - pstart[e_row]
    src = jnp.clip(start[e_row] + i_in, 0, a - 1)
    buf_tok = jnp.where((i_in >= 0) & (i_in < counts[e_row]), tok_flat[order][src], 0)
    pos = dest[jnp.argsort(order)]
    tok_blocks = buf_tok.reshape(n_blk, 1, tb)
    pos_blocks = pos.reshape(n // tm, tm, EXPERT_TOPK).transpose(0, 2, 1).reshape(
        n // tm, 1, EXPERT_TOPK * tm)
    return tok_blocks, blk_e, nvalid, pos_blocks


def _pick_tile(n, pref):
    t = pref
    while n % t:
        t //= 2
    return t


def kernel(x, norm1_g, w_in, b_gates, w_out_moba, w_out_dil, w_o, norm2_g, w_group, b_group,
           w_expert, b_expert, w_gate, w_up, w_down, norm_f_g):
    bsz, seq, d = x.shape
    n = bsz * seq
    assert w_in.shape[0] == 1, "one layer: the final RMSNorm is fused into the combine stage"
    width = w_out_moba.shape[1]
    assert w_out_dil.shape[1] == width and width % HEAD_DIM == 0
    n_heads = width // HEAD_DIM
    n_groups = w_group.shape[-1]
    n_experts = w_expert.shape[-1]
    per_group = n_experts // n_groups
    assert n_groups + n_experts <= LANES
    slopes_m, slopes_d = _alibi_slopes(n_heads, n_heads)
    tm_in = _pick_tile(n, 512)
    tm_out = _pick_tile(n, 256)
    tb = 256
    tm_c = _pick_tile(n, 256)
    l = 0

    x2 = x.reshape(n, d)
    projm, projd, gates = _inproj(x2, norm1_g[l], w_in[l].astype(BF16), b_gates[l], width, tm_in)
    o_m = _moba(projm, slopes_m, bsz, seq, n_heads)
    o_d = _dilated(projd, slopes_d, bsz, seq, n_heads)

    w_r = jnp.concatenate([w_group[l], w_expert[l]], axis=1)
    w_r = jnp.pad(w_r, ((0, 0), (0, LANES - w_r.shape[1])))
    wr_hi = w_r.astype(BF16)
    wr_lo = (w_r - wr_hi.astype(F32)).astype(BF16)
    b_r = jnp.pad(jnp.concatenate([b_group[l], b_expert[l]]),
                  (0, LANES - n_groups - n_experts)).reshape(1, LANES)
    x1, h2, rt = _outproj(o_m, o_d, gates, x2, w_out_moba[l].astype(BF16),
                          w_out_dil[l].astype(BF16), w_o[l].astype(BF16), norm2_g[l],
                          wr_hi, wr_lo, b_r, n_groups, per_group, tm_out)

    expert_id = rt[:, 2:2 + EXPERT_TOPK].astype(jnp.int32)
    tok_blocks, blk_e, nvalid, pos_blocks = _dispatch_tables(expert_id, n_experts, tb, tm_c)
    wgu = jnp.concatenate([w_gate[l], w_up[l]], axis=-1).astype(BF16)
    y = _experts(h2, tok_blocks, blk_e, nvalid, wgu, w_down[l].astype(BF16), tb)
    out = _combine(x1, rt, norm_f_g, y, pos_blocks, tm_c)
    return out.reshape(bsz, seq, d)
```

```python
import functools

import numpy as np
import jax
import jax.numpy as jnp
from jax import lax
from jax.experimental import pallas as pl
from jax.experimental.pallas import tpu as pltpu

HEAD_DIM = 128
MOBA_BLOCK = 256
MOBA_TOPK = 3
DIL_PATTERNS = ((128, 1), (512, 4), (2048, 16))
DIL_STEPS = 128
DIL_CHUNK = 2048
DIL_BATCH = 8
EXPERT_TOPK = 2
RMS_EPS = 1e-6
NEG = -1e30
LOG2E = 1.4426950408889634
LANES = 128
VMEM_LIMIT = 56 * 1024 * 1024

F32 = jnp.float32
BF16 = jnp.bfloat16


def _alibi_slopes(n_moba, n_dil):
    n = n_moba + n_dil
    s = 2.0 ** (-8.0 * np.arange(1, n + 1) / n)
    return jnp.asarray(s[0::2], F32), jnp.asarray(s[1::2], F32)


def _inproj_kernel(x_ref, g_ref, w_ref, b_ref, om_ref, od_ref, og_ref, h_scr, *, qscale):
    j = pl.program_id(1)

    @pl.when(j == 0)
    def _():
        x = x_ref[...]
        ms = jnp.mean(x * x, axis=-1, keepdims=True)
        h_scr[...] = (x * lax.rsqrt(ms + RMS_EPS) * g_ref[...]).astype(BF16)

    acc = jnp.dot(h_scr[...], w_ref[...], preferred_element_type=F32)

    @pl.when(j == 0)
    def _():
        om_ref[...] = (acc * qscale).astype(BF16)

    @pl.when((j > 0) & (j < 3))
    def _():
        om_ref[...] = acc.astype(BF16)

    @pl.when(j == 3)
    def _():
        od_ref[...] = acc * qscale

    @pl.when((j > 3) & (j < 6))
    def _():
        od_ref[...] = acc

    @pl.when(j >= 6)
    def _():
        og_ref[...] = jax.nn.sigmoid(acc + b_ref[...]).astype(BF16)


def _inproj(x2, g1, w_in, b_gates, width, tm):
    n, d = x2.shape
    cols = w_in.shape[1]
    tn = width
    assert cols == 6 * width + 2 * d and d % tn == 0 and n % tm == 0
    n_gate = (2 * d) // tn
    qscale = HEAD_DIM ** -0.5 * LOG2E
    return pl.pallas_call(
        functools.partial(_inproj_kernel, qscale=qscale),
        grid=(n // tm, 6 + n_gate),
        in_specs=[
            pl.BlockSpec((tm, d), lambda i, j: (i, 0)),
            pl.BlockSpec((1, d), lambda i, j: (0, 0)),
            pl.BlockSpec((d, tn), lambda i, j: (0, j)),
            pl.BlockSpec((1, tn), lambda i, j: (0, jnp.clip(j - 6, 0, n_gate - 1))),
        ],
        out_specs=[
            pl.BlockSpec((tm, tn), lambda i, j: (i, jnp.minimum(j, 2))),
            pl.BlockSpec((tm, tn), lambda i, j: (i, jnp.clip(j - 3, 0, 2))),
            pl.BlockSpec((tm, tn), lambda i, j: (i, jnp.clip(j - 6, 0, n_gate - 1))),
        ],
        out_shape=[
            jax.ShapeDtypeStruct((n, 3 * width), BF16),
            jax.ShapeDtypeStruct((n, 3 * width), F32),
            jax.ShapeDtypeStruct((n, 2 * d), BF16),
        ],
        scratch_shapes=[pltpu.VMEM((tm, d), BF16)],
        compiler_params=pltpu.CompilerParams(
            dimension_semantics=("arbitrary", "arbitrary"), vmem_limit_bytes=VMEM_LIMIT),
        name="inproj",
    )(x2, g1.reshape(1, d), w_in, b_gates.reshape(1, 2 * d))


def _nt_dot(a, b):
    return lax.dot_general(a, b, (((1,), (1,)), ((), ())), preferred_element_type=F32)


def _dot(a, b):
    return jnp.dot(a, b, preferred_element_type=F32)


MOBA_TILE = 2 * MOBA_BLOCK
ALIBI_SPLIT = 3
VT_ROWS = HEAD_DIM + 16


def _moba_kernel(slopes_ref, q_ref, k_ref, v_ref, o_ref, kaug, vt, kmh, kml, causal, sbuf, pbuf,
                 *, nb, nbp, hpg):
    blk = MOBA_BLOCK
    hd = HEAD_DIM
    tq = tk = MOBA_TILE
    hg = pl.program_id(1)
    qp = pl.program_id(2)
    heads = range(hpg)
    slope2 = [slopes_ref[hg * hpg + g] * LOG2E for g in heads]
    cols = [slice(g * hd, (g + 1) * hd) for g in heads]
    s_len = k_ref.shape[0]
    c_hi0 = nbp + ALIBI_SPLIT

    @pl.when(qp == 0)
    def _init():
        pos = lax.broadcasted_iota(jnp.int32, (s_len, LANES), 0)
        lane = lax.broadcasted_iota(jnp.int32, (s_len, LANES), 1)
        off = pos % tk
        aug = jnp.where(lane < nbp, jnp.where(pos // blk == lane, 1.0, 0.0),
                        jnp.where(lane < c_hi0, (off % blk).astype(F32),
                                  jnp.where(lane < c_hi0 + ALIBI_SPLIT,
                                            (off - off % blk).astype(F32), 0.0))).astype(BF16)
        row = lax.broadcasted_iota(jnp.int32, (VT_ROWS - hd, s_len), 0)
        ones_row = jnp.where(row == 0, 1.0, 0.0).astype(BF16)
        for g in heads:
            kaug[g, :, 0:hd] = k_ref[:, cols[g]]
            kaug[g, :, hd:hd + LANES] = aug

            def xpose(n, carry, g=g):
                r0 = pl.multiple_of(n * blk, blk)
                vt[g, 0:hd, pl.ds(r0, blk)] = (
                    v_ref[pl.ds(r0, blk), cols[g]].astype(F32).T.astype(BF16))
                return carry

            lax.fori_loop(0, nb, xpose, 0)
            vt[g, hd:VT_ROWS, :] = ones_row
            km = jnp.sum(k_ref[:, cols[g]].astype(F32).reshape(nb, blk, hd), axis=1) * (1.0 / blk)
            if nb < nbp:
                km = jnp.concatenate([km, jnp.zeros((nbp - nb, hd), F32)], axis=0)
            hi = km.astype(BF16)
            kmh[g] = hi
            kml[g] = (km - hi.astype(F32)).astype(BF16)
        cc = lax.broadcasted_iota(jnp.int32, (tk, tq), 0)
        rr = lax.broadcasted_iota(jnp.int32, (tk, tq), 1)
        causal[...] = jnp.where(cc > rr, NEG, 0.0)

    lane_q = lax.broadcasted_iota(jnp.int32, (1, tq), 1)
    qblk = qp * (tq // blk) + lane_q // blk
    t_q = qp * tq + lane_q
    bidx = lax.broadcasted_iota(jnp.int32, (nbp, tq), 0)
    ri = lax.broadcasted_iota(jnp.int32, (16, tq), 0)
    tail = jnp.zeros((LANES - nbp - 16, tq), BF16)

    def augmented_queries(g):
        qt = q_ref[:, cols[g]].astype(F32).T.astype(BF16)
        gate = _dot(kmh[g], qt) + _dot(kml[g], qt)
        gt = jnp.where(bidx < qblk, gate, -jnp.inf)
        selneg = jnp.full((nbp, tq), NEG, F32)
        for _ in range(min(MOBA_TOPK, nb)):
            mx = jnp.max(gt, axis=0, keepdims=True)
            idx = jnp.min(jnp.where(gt == mx, bidx, nbp), axis=0, keepdims=True)
            hit = bidx == idx
            selneg = jnp.where(hit & (mx > -jnp.inf), 0.0, selneg)
            gt = jnp.where(hit, -jnp.inf, gt)
        sv = jnp.full((16, tq), slope2[g], F32)
        pieces = []
        for _ in range(ALIBI_SPLIT):
            pc = sv.astype(BF16).astype(F32)
            pieces.append(pc)
            sv = sv - pc
        srows = jnp.zeros((16, tq), F32)
        for j in range(2 * ALIBI_SPLIT):
            srows = jnp.where(ri == j, pieces[j % ALIBI_SPLIT], srows)

        def augment(sel):
            return jnp.concatenate([qt, sel.astype(BF16), srows.astype(BF16), tail], axis=0)

        return augment(selneg), augment(jnp.where(bidx == qblk, 0.0, selneg))

    qaugs = [augmented_queries(g) for g in heads]

    def scores(g, k0, qa):
        return _dot(kaug[g, pl.ds(k0, tk), :], qa)

    def softmax(g, k0, raw, m):
        shift = slope2[g] * (k0 - t_q).astype(F32)
        m_new = jnp.maximum(m, jnp.max(raw, axis=0, keepdims=True) + shift)
        return m_new, jnp.exp2(raw - (m_new - shift)), jnp.exp2(m - m_new)

    def values(g, k0):
        return _dot(vt[g, :, pl.ds(k0, tk)], pbuf[g])

    for g in heads:
        sbuf[g] = scores(g, pl.multiple_of(qp * tk, tk), qaugs[g][1]) + causal[...]
        pbuf[g] = jnp.zeros((tk, tq), BF16)

    def step(j, st):
        k_cur = pl.multiple_of((qp - j) * tk, tk)
        k_prev = pl.multiple_of(jnp.minimum(k_cur + tk, qp * tk), tk)
        raws = [scores(g, k_cur - tk, qaugs[g][0]) for g in heads]
        pvs = [values(g, k_prev) for g in heads]
        out = []
        for g in heads:
            m, alpha_prev, acc = st[g]
            m_new, p, alpha = softmax(g, k_cur, sbuf[g], m)
            acc = alpha_prev * acc + pvs[g]
            pbuf[g] = p.astype(BF16)
            sbuf[g] = raws[g]
            out.append((m_new, alpha, acc))
        return tuple(out)

    init = (jnp.full((1, tq), -jnp.inf, F32), jnp.ones((1, tq), F32),
            jnp.zeros((VT_ROWS, tq), F32))
    fin = lax.fori_loop(0, qp, step, tuple(init for _ in heads))
    k_last = pl.multiple_of(jnp.minimum(tk, qp * tk), tk)
    for g in heads:
        m, alpha_prev, acc = fin[g]
        acc = alpha_prev * acc + values(g, k_last)
        _, p, alpha = softmax(g, 0, sbuf[g], m)
        pbuf[g] = p.astype(BF16)
        acc = alpha * acc + values(g, 0)
        o_ref[:, cols[g]] = (acc[0:hd, :] / acc[hd:hd + 1, :]).T.astype(o_ref.dtype)


def _moba(projm, slopes, bsz, seq, n_heads):
    width = n_heads * HEAD_DIM
    nb = seq // MOBA_BLOCK
    nbp = -(-nb // 16) * 16
    tile = MOBA_TILE
    hpg = 2 if n_heads % 2 == 0 else 1
    ng = n_heads // hpg
    gw = hpg * HEAD_DIM
    assert seq % tile == 0 and nbp + 16 <= LANES
    pm = projm.reshape(bsz, seq, 3 * width)
    grid_spec = pltpu.PrefetchScalarGridSpec(
        num_scalar_prefetch=1,
        grid=(bsz, ng, seq // tile),
        in_specs=[
            pl.BlockSpec((None, tile, gw), lambda b, h, i, s: (b, i, h)),
            pl.BlockSpec((None, seq, gw), lambda b, h, i, s: (b, 0, ng + h)),
            pl.BlockSpec((None, seq, gw), lambda b, h, i, s: (b, 0, 2 * ng + h)),
        ],
        out_specs=pl.BlockSpec((None, tile, gw), lambda b, h, i, s: (b, i, h)),
        scratch_shapes=[
            pltpu.VMEM((hpg, seq, HEAD_DIM + LANES), BF16),
            pltpu.VMEM((hpg, VT_ROWS, seq), BF16),
            pltpu.VMEM((hpg, nbp, HEAD_DIM), BF16),
            pltpu.VMEM((hpg, nbp, HEAD_DIM), BF16),
            pltpu.VMEM((tile, tile), F32),
            pltpu.VMEM((hpg, tile, tile), F32),
            pltpu.VMEM((hpg, tile, tile), BF16),
        ],
    )
    out = pl.pallas_call(
        functools.partial(_moba_kernel, nb=nb, nbp=nbp, hpg=hpg),
        grid_spec=grid_spec,
        out_shape=jax.ShapeDtypeStruct((bsz, seq, width), BF16),
        compiler_params=pltpu.CompilerParams(
            dimension_semantics=("arbitrary", "arbitrary", "arbitrary"),
            vmem_limit_bytes=VMEM_LIMIT),
        name="moba",
    )(slopes, pm, pm, pm)
    return out.reshape(bsz * seq, width)


def _dil_kernel(slopes_ref, q_ref, kc_ref, kp_ref, vc_ref, vp_ref, o_ref, bias, osc, lsc):
    w = DIL_STEPS
    chunk = q_ref.shape[0]
    h = pl.program_id(1)
    c = pl.program_id(2)
    slope2 = slopes_ref[h] * LOG2E

    @pl.when(c == 0)
    def _init():
        i = lax.broadcasted_iota(jnp.int32, (w, 2 * w), 0)
        j = lax.broadcasted_iota(jnp.int32, (w, 2 * w), 1)
        diff = i + w - j
        ok = (diff >= 0) & (diff <= w)
        for p, (_, d) in enumerate(DIL_PATTERNS):
            bias[p] = jnp.where(ok, diff.astype(F32) * (-slope2 * d), NEG)

    jj = lax.broadcasted_iota(jnp.int32, (w, 2 * w), 1)
    first_pen = jnp.where(jj < w, jnp.where(c > 0, 0.0, NEG), 0.0)

    def rows(start, d):
        return pl.ds(start, w) if d == 1 else pl.ds(start, w, stride=d)

    def tiles(p, d, qstarts, kprev_ref, vprev_ref, pstarts, from_prev_chunk):
        n = len(qstarts)
        ss = []
        for qstart, pstart in zip(qstarts, pstarts):
            q = q_ref[rows(qstart, d), :].astype(BF16)
            k2 = jnp.concatenate([kprev_ref[rows(pstart, d), :], kc_ref[rows(qstart, d), :]],
                                 axis=0).astype(BF16)
            ss.append(_nt_dot(q, k2))
        prs, ms, ls = [], [], []
        for s in ss:
            s = s + bias[p]
            if from_prev_chunk:
                s = s + first_pen
            m = jnp.max(s, axis=-1, keepdims=True)
            pr = jnp.exp2(s - m)
            prs.append(pr.astype(BF16))
            ms.append(m)
            ls.append(jnp.sum(pr, axis=-1, keepdims=True))
        for i in range(n):
            v2 = jnp.concatenate([vprev_ref[rows(pstarts[i], d), :],
                                  vc_ref[rows(qstarts[i], d), :]], axis=0).astype(BF16)
            o = jnp.dot(prs[i], v2, preferred_element_type=F32) / ls[i]
            osc[p, rows(qstarts[i], d), :] = o
            lsc[p, rows(qstarts[i], d), :] = jnp.broadcast_to(ms[i] + jnp.log2(ls[i]), (w, LANES))

    def batch_size(n):
        return max(g for g in range(1, DIL_BATCH + 1) if n % g == 0)

    for p, (_, d) in enumerate(DIL_PATTERNS):
        span = w * d
        n_sb = chunk // span

        g1 = batch_size(d)

        def first(t, carry, p=p, d=d, span=span, g1=g1):
            rs = [t * g1 + i for i in range(g1)]
            tiles(p, d, rs, kp_ref, vp_ref, [chunk - span + r for r in rs], True)
            return carry

        lax.fori_loop(0, d // g1, first, 0)

        n_rest = (n_sb - 1) * d
        if n_rest:
            g2 = batch_size(n_rest)

            def rest(t, carry, p=p, d=d, span=span, g2=g2):
                idx = [t * g2 + i for i in range(g2)]
                qstarts = [(1 + x // d) * span + x % d for x in idx]
                if d == 1:
                    qstarts = [pl.multiple_of(x, w) for x in qstarts]
                tiles(p, d, qstarts, kc_ref, vc_ref, [x - span for x in qstarts], False)
                return carry

            lax.fori_loop(0, n_rest // g2, rest, 0)

    rb = 256

    def merge(t, carry):
        r0 = pl.multiple_of(t * rb, rb)
        ls = [lsc[p, pl.ds(r0, rb), :] for p in range(len(DIL_PATTERNS))]
        mx = functools.reduce(jnp.maximum, ls)
        ws = [jnp.exp2(x - mx) for x in ls]
        num = ws[0] * osc[0, pl.ds(r0, rb), :]
        den = ws[0]
        for p in range(1, len(DIL_PATTERNS)):
            num = num + ws[p] * osc[p, pl.ds(r0, rb), :]
            den = den + ws[p]
        o_ref[pl.ds(r0, rb), :] = (num / den).astype(o_ref.dtype)
        return carry

    lax.fori_loop(0, chunk // rb, merge, 0)


def _dilated(projd, slopes, bsz, seq, n_heads):
    width = n_heads * HEAD_DIM
    chunk = DIL_CHUNK
    assert seq % chunk == 0
    assert all(win // d == DIL_STEPS and chunk % win == 0 for win, d in DIL_PATTERNS)
    pd_ = projd.reshape(bsz, seq, 3 * width)
    blk = (None, chunk, HEAD_DIM)
    npat = len(DIL_PATTERNS)
    grid_spec = pltpu.PrefetchScalarGridSpec(
        num_scalar_prefetch=1,
        grid=(bsz, n_heads, seq // chunk),
        in_specs=[
            pl.BlockSpec(blk, lambda b, h, c, s: (b, c, h)),
            pl.BlockSpec(blk, lambda b, h, c, s: (b, c, n_heads + h)),
            pl.BlockSpec(blk, lambda b, h, c, s: (b, jnp.maximum(c - 1, 0), n_heads + h)),
            pl.BlockSpec(blk, lambda b, h, c, s: (b, c, 2 * n_heads + h)),
            pl.BlockSpec(blk, lambda b, h, c, s: (b, jnp.maximum(c - 1, 0), 2 * n_heads + h)),
        ],
        out_specs=pl.BlockSpec(blk, lambda b, h, c, s: (b, c, h)),
        scratch_shapes=[
            pltpu.VMEM((npat, DIL_STEPS, 2 * DIL_STEPS), F32),
            pltpu.VMEM((npat, chunk, HEAD_DIM), F32),
            pltpu.VMEM((npat, chunk, LANES), F32),
        ],
    )
    out = pl.pallas_call(
        _dil_kernel,
        grid_spec=grid_spec,
        out_shape=jax.ShapeDtypeStruct((bsz, seq, width), BF16),
        compiler_params=pltpu.CompilerParams(
            dimension_semantics=("arbitrary", "arbitrary", "arbitrary"),
            vmem_limit_bytes=VMEM_LIMIT),
        name="dilated",
    )(slopes, pd_, pd_, pd_, pd_, pd_)
    return out.reshape(bsz * seq, width)


def _store_slabs(ref, val):
    rows, d = val.shape
    slabs = d // LANES
    for s in range(slabs):
        ref[pl.ds(s, rows, stride=slabs), :] = val[:, s * LANES:(s + 1) * LANES]


def _load_slab(ref3, slot, first_row, rows, slabs, s):
    return ref3[slot, pl.ds(first_row * slabs + s, rows, stride=slabs), :]


def _outproj_kernel(am_ref, ad_ref, gm_ref, gd_ref, x_ref, wm_ref, wd_ref, wo_ref, g2_ref,
                    wrh_ref, wrl_ref, br_ref, x1_ref, h2_ref, rt_ref, *, n_groups, per_group):
    y_m = jnp.dot(am_ref[...], wm_ref[...], preferred_element_type=F32)
    y_d = jnp.dot(ad_ref[...], wd_ref[...], preferred_element_type=F32)
    z = (gm_ref[...].astype(F32) * y_m + gd_ref[...].astype(F32) * y_d).astype(BF16)
    x1 = x_ref[...] + jnp.dot(z, wo_ref[...], preferred_element_type=F32)
    x1_ref[...] = x1
    ms = jnp.mean(x1 * x1, axis=-1, keepdims=True)
    h2 = x1 * lax.rsqrt(ms + RMS_EPS) * g2_ref[...]
    _store_slabs(h2_ref, h2)

    hi = h2.astype(BF16)
    lo = (h2 - hi.astype(F32)).astype(BF16)
    logits = (jnp.dot(hi, wrh_ref[...], preferred_element_type=F32)
              + (jnp.dot(hi, wrl_ref[...], preferred_element_type=F32)
                 + jnp.dot(lo, wrh_ref[...], preferred_element_type=F32))
              + br_ref[...])

    lane = lax.broadcasted_iota(jnp.int32, logits.shape, 1)
    big = jnp.int32(1 << 20)
    isg = lane < n_groups
    gl = jnp.where(isg, logits, -jnp.inf)
    gex = jnp.exp(gl - jnp.max(gl, axis=-1, keepdims=True))
    gprob = gex / jnp.sum(gex, axis=-1, keepdims=True)
    gtp = jnp.max(gprob, axis=-1, keepdims=True)
    gtop = jnp.min(jnp.where((gprob == gtp) & isg, lane, big), axis=-1, keepdims=True)
    e_lo = n_groups + gtop * per_group
    ise = (lane >= e_lo) & (lane < e_lo + per_group)
    el = jnp.where(ise, logits, -jnp.inf)
    eex = jnp.exp(el - jnp.max(el, axis=-1, keepdims=True))
    eprob = jnp.where(ise, eex / jnp.sum(eex, axis=-1, keepdims=True), -1.0)
    p1 = jnp.max(eprob, axis=-1, keepdims=True)
    i1 = jnp.min(jnp.where(eprob == p1, lane, big), axis=-1, keepdims=True)
    eprob2 = jnp.where(lane == i1, -1.0, eprob)
    p2 = jnp.max(eprob2, axis=-1, keepdims=True)
    i2 = jnp.min(jnp.where(eprob2 == p2, lane, big), axis=-1, keepdims=True)
    den = p1 + p2
    w1 = gtp * p1 / den
    w2 = gtp * p2 / den
    e1 = (i1 - n_groups).astype(F32)
    e2 = (i2 - n_groups).astype(F32)
    rt_ref[...] = jnp.where(lane == 0, w1, jnp.where(lane == 1, w2, jnp.where(
        lane == 2, e1, jnp.where(lane == 3, e2, 0.0))))


def _outproj(a_m, a_d, gates, x2, w_m, w_d, w_o, g2, wr_hi, wr_lo, b_r, n_groups, per_group, tm):
    n, d = x2.shape
    width = a_m.shape[1]
    assert n % tm == 0

    def const(shape):
        return pl.BlockSpec(shape, lambda i: (0, 0))

    return pl.pallas_call(
        functools.partial(_outproj_kernel, n_groups=n_groups, per_group=per_group),
        grid=(n // tm,),
        in_specs=[
            pl.BlockSpec((tm, width), lambda i: (i, 0)),
            pl.BlockSpec((tm, width), lambda i: (i, 0)),
            pl.BlockSpec((tm, d), lambda i: (i, 0)),
            pl.BlockSpec((tm, d), lambda i: (i, 1)),
            pl.BlockSpec((tm, d), lambda i: (i, 0)),
            const((width, d)), const((width, d)), const((d, d)), const((1, d)),
            const((d, LANES)), const((d, LANES)), const((1, LANES)),
        ],
        out_specs=[
            pl.BlockSpec((tm, d), lambda i: (i, 0)),
            pl.BlockSpec((tm * (d // LANES), LANES), lambda i: (i, 0)),
            pl.BlockSpec((tm, LANES), lambda i: (i, 0)),
        ],
        out_shape=[
            jax.ShapeDtypeStruct((n, d), F32),
            jax.ShapeDtypeStruct((n * (d // LANES), LANES), F32),
            jax.ShapeDtypeStruct((n, LANES), F32),
        ],
        compiler_params=pltpu.CompilerParams(
            dimension_semantics=("arbitrary",), vmem_limit_bytes=VMEM_LIMIT),
        name="outproj",
    )(a_m, a_d, gates, gates, x2, w_m, w_d, w_o, g2.reshape(1, d), wr_hi, wr_lo, b_r)


DMA_PRIORITIES = 2


def _start_row_gather(src_hbm, idx_ref, dst, sem, n_rows, slabs):
    def issue(t, carry):
        for k in range(DMA_PRIORITIES):
            r = t * DMA_PRIORITIES + k
            src_row = pl.multiple_of(idx_ref[0, 0, r] * slabs, slabs)
            pltpu.make_async_copy(src_hbm.at[pl.ds(src_row, slabs), :],
                                  dst.at[pl.ds(pl.multiple_of(r * slabs, slabs), slabs), :],
                                  sem).start(priority=k)
        return carry

    assert n_rows % DMA_PRIORITIES == 0
    lax.fori_loop(0, n_rows // DMA_PRIORITIES, issue, 0)


def _wait_row_gather(src_hbm, dst, sem, n_rows, slabs):
    pltpu.make_async_copy(src_hbm.at[pl.ds(0, n_rows * slabs), :], dst, sem).wait()


def _experts_kernel(blk_e_ref, nvalid_ref, tok_ref, tok_next_ref, h_hbm, wgu_ref, wd_ref, y_ref,
                    xbuf, sem, *, d_expert, tb):
    del blk_e_ref
    i = pl.program_id(0)
    nvalid = nvalid_ref[0]
    slabs = xbuf.shape[1] // tb
    slot = i % 2

    @pl.when((i == 0) & (i < nvalid))
    def _():
        _start_row_gather(h_hbm, tok_ref, xbuf.at[0], sem.at[0], tb, slabs)

    @pl.when(i + 1 < nvalid)
    def _():
        _start_row_gather(h_hbm, tok_next_ref, xbuf.at[1 - slot], sem.at[1 - slot], tb, slabs)

    @pl.when(i < nvalid)
    def _():
        _wait_row_gather(h_hbm, xbuf.at[slot], sem.at[slot], tb, slabs)
        x = jnp.concatenate([_load_slab(xbuf, slot, 0, tb, slabs, s).astype(BF16)
                             for s in range(slabs)], axis=1)
        gu = jnp.dot(x, wgu_ref[...], preferred_element_type=F32)
        a = (jax.nn.silu(gu[:, :d_expert]) * gu[:, d_expert:]).astype(BF16)
        _store_slabs(y_ref, jnp.dot(a, wd_ref[...], preferred_element_type=F32))

    @pl.when(i >= nvalid)
    def _():
        y_ref[...] = jnp.zeros_like(y_ref)


def _experts(h2s, tok_blocks, blk_e, nvalid, wgu, wd, tb):
    d = wgu.shape[1]
    slabs = d // LANES
    n_blk = tok_blocks.shape[0]
    d_expert = wd.shape[1]
    grid_spec = pltpu.PrefetchScalarGridSpec(
        num_scalar_prefetch=2,
        grid=(n_blk,),
        in_specs=[
            pl.BlockSpec((1, 1, tb), lambda i, be, nv: (i, 0, 0), memory_space=pltpu.SMEM),
            pl.BlockSpec((1, 1, tb), lambda i, be, nv: (jnp.minimum(i + 1, n_blk - 1), 0, 0),
                         memory_space=pltpu.SMEM),
            pl.BlockSpec(memory_space=pl.ANY),
            pl.BlockSpec((None, d, 2 * d_expert), lambda i, be, nv: (be[i], 0, 0)),
            pl.BlockSpec((None, d_expert, d), lambda i, be, nv: (be[i], 0, 0)),
        ],
        out_specs=pl.BlockSpec((tb * slabs, LANES), lambda i, be, nv: (i, 0)),
        scratch_shapes=[
            pltpu.VMEM((2, tb * slabs, LANES), F32),
            pltpu.SemaphoreType.DMA((2,)),
        ],
    )
    return pl.pallas_call(
        functools.partial(_experts_kernel, d_expert=d_expert, tb=tb),
        grid_spec=grid_spec,
        out_shape=jax.ShapeDtypeStruct((n_blk * tb * slabs, LANES), F32),
        compiler_params=pltpu.CompilerParams(
            dimension_semantics=("arbitrary",), vmem_limit_bytes=VMEM_LIMIT),
        name="experts",
    )(blk_e, nvalid, tok_blocks, tok_blocks, h2s, wgu, wd)


def _combine_kernel(pos_ref, pos_next_ref, x1_ref, rt_ref, gf_ref, y_hbm, o_ref, ybuf, sem):
    i = pl.program_id(0)
    n_steps = pl.num_programs(0)
    tm = x1_ref.shape[0]
    slot = i % 2

    slabs = x1_ref.shape[1] // LANES

    @pl.when(i == 0)
    def _():
        _start_row_gather(y_hbm, pos_ref, ybuf.at[0], sem.at[0], 2 * tm, slabs)

    @pl.when(i + 1 < n_steps)
    def _():
        _start_row_gather(y_hbm, pos_next_ref, ybuf.at[1 - slot], sem.at[1 - slot], 2 * tm, slabs)

    _wait_row_gather(y_hbm, ybuf.at[slot], sem.at[slot], 2 * tm, slabs)
    rt = rt_ref[...]
    x = jnp.concatenate(
        [x1_ref[:, s * LANES:(s + 1) * LANES]
         + (rt[:, 0:1] * _load_slab(ybuf, slot, 0, tm, slabs, s)
            + rt[:, 1:2] * _load_slab(ybuf, slot, tm, tm, slabs, s))
         for s in range(slabs)], axis=1)
    ms = jnp.mean(x * x, axis=-1, keepdims=True)
    o_ref[...] = x * lax.rsqrt(ms + RMS_EPS) * gf_ref[...]


def _combine(x1, rt, gf, y, pos_blocks, tm):
    n, d = x1.shape
    n_steps = n // tm
    return pl.pallas_call(
        _combine_kernel,
        grid=(n_steps,),
        in_specs=[
            pl.BlockSpec((1, 1, 2 * tm), lambda i: (i, 0, 0), memory_space=pltpu.SMEM),
            pl.BlockSpec((1, 1, 2 * tm), lambda i: (jnp.minimum(i + 1, n_steps - 1), 0, 0),
                         memory_space=pltpu.SMEM),
            pl.BlockSpec((tm, d), lambda i: (i, 0)),
            pl.BlockSpec((tm, LANES), lambda i: (i, 0)),
            pl.BlockSpec((1, d), lambda i: (0, 0)),
            pl.BlockSpec(memory_space=pl.ANY),
        ],
        out_specs=pl.BlockSpec((tm, d), lambda i: (i, 0)),
        out_shape=jax.ShapeDtypeStruct((n, d), F32),
        scratch_shapes=[
            pltpu.VMEM((2, 2 * tm * (d // LANES), LANES), F32),
            pltpu.SemaphoreType.DMA((2,)),
        ],
        compiler_params=pltpu.CompilerParams(
            dimension_semantics=("arbitrary",), vmem_limit_bytes=VMEM_LIMIT),
        name="combine",
    )(pos_blocks, pos_blocks, x1, rt, gf.reshape(1, d), y)


def _dispatch_tables(expert_id, n_experts, tb, tm):
    n = expert_id.shape[0]
    a = n * EXPERT_TOPK
    e_flat = expert_id.reshape(a)
    tok_flat = jnp.arange(a, dtype=jnp.int32) // EXPERT_TOPK
    order = jnp.argsort(e_flat)
    e_sorted = e_flat[order]
    counts = jnp.bincount(e_flat, length=n_experts).astype(jnp.int32)
    padded = ((counts + tb - 1) // tb) * tb
    start = jnp.cumsum(counts) - counts
    pend = jnp.cumsum(padded)
    pstart = pend - padded
    dest = pstart[e_sorted] + (jnp.arange(a, dtype=jnp.int32) - start[e_sorted])
    n_blk = (a + n_experts * tb) // tb
    blk_start = jnp.arange(n_blk, dtype=jnp.int32) * tb
    blk_e = jnp.clip(jnp.sum(pend[None, :] <= blk_start[:, None], axis=1), 0,
                     n_experts - 1).astype(jnp.int32)
    nvalid = (pend[-1] // tb).astype(jnp.int32).reshape(1)
    row = jnp.arange(n_blk * tb, dtype=jnp.int32)
    e_row = jnp.repeat(blk_e, tb)
    i_in = row - pstart[e_row]
    src = jnp.clip(start[e_row] + i_in, 0, a - 1)
    buf_tok = jnp.where((i_in >= 0) & (i_in < counts[e_row]), tok_flat[order][src], 0)
    pos = dest[jnp.argsort(order)]
    tok_blocks = buf_tok.reshape(n_blk, 1, tb)
    pos_blocks = pos.reshape(n // tm, tm, EXPERT_TOPK).transpose(0, 2, 1).reshape(
        n // tm, 1, EXPERT_TOPK * tm)
    return tok_blocks, blk_e, nvalid, pos_blocks


def _pick_tile(n, pref):
    t = pref
    while n % t:
        t //= 2
    return t


def kernel(x, norm1_g, w_in, b_gates, w_out_moba, w_out_dil, w_o, norm2_g, w_group, b_group,
           w_expert, b_expert, w_gate, w_up, w_down, norm_f_g):
    bsz, seq, d = x.shape
    n = bsz * seq
    assert w_in.shape[0] == 1, "one layer: the final RMSNorm is fused into the combine stage"
    width = w_out_moba.shape[1]
    assert w_out_dil.shape[1] == width and width % HEAD_DIM == 0
    n_heads = width // HEAD_DIM
    n_groups = w_group.shape[-1]
    n_experts = w_expert.shape[-1]
    per_group = n_experts // n_groups
    assert n_groups + n_experts <= LANES
    slopes_m, slopes_d = _alibi_slopes(n_heads, n_heads)
    tm_in = _pick_tile(n, 512)
    tm_out = _pick_tile(n, 256)
    tb = 256
    tm_c = _pick_tile(n, 256)
    l = 0

    x2 = x.reshape(n, d)
    projm, projd, gates = _inproj(x2, norm1_g[l], w_in[l].astype(BF16), b_gates[l], width, tm_in)
    o_m = _moba(projm, slopes_m, bsz, seq, n_heads)
    o_d = _dilated(projd, slopes_d, bsz, seq, n_heads)

    w_r = jnp.concatenate([w_group[l], w_expert[l]], axis=1)
    w_r = jnp.pad(w_r, ((0, 0), (0, LANES - w_r.shape[1])))
    wr_hi = w_r.astype(BF16)
    wr_lo = (w_r - wr_hi.astype(F32)).astype(BF16)
    b_r = jnp.pad(jnp.concatenate([b_group[l], b_expert[l]]),
                  (0, LANES - n_groups - n_experts)).reshape(1, LANES)
    x1, h2, rt = _outproj(o_m, o_d, gates, x2, w_out_moba[l].astype(BF16),
                          w_out_dil[l].astype(BF16), w_o[l].astype(BF16), norm2_g[l],
                          wr_hi, wr_lo, b_r, n_groups, per_group, tm_out)

    expert_id = rt[:, 2:2 + EXPERT_TOPK].astype(jnp.int32)
    tok_blocks, blk_e, nvalid, pos_blocks = _dispatch_tables(expert_id, n_experts, tb, tm_c)
    wgu = jnp.concatenate([w_gate[l], w_up[l]], axis=-1).astype(BF16)
    y = _experts(h2, tok_blocks, blk_e, nvalid, wgu, w_down[l].astype(BF16), tb)
    out = _combine(x1, rt, norm_f_g, y, pos_blocks, tm_c)
    return out.reshape(bsz, seq, d)
```

```python
import functools

import numpy as np
import jax
import jax.numpy as jnp
from jax import lax
from jax.experimental import pallas as pl
from jax.experimental.pallas import tpu as pltpu

HEAD_DIM = 128
MOBA_BLOCK = 256
MOBA_TOPK = 3
DIL_PATTERNS = ((128, 1), (512, 4), (2048, 16))
DIL_STEPS = 128
DIL_CHUNK = 2048
DIL_BATCH = 8
EXPERT_TOPK = 2
RMS_EPS = 1e-6
NEG = -1e30
LOG2E = 1.4426950408889634
LANES = 128
VMEM_LIMIT = 56 * 1024 * 1024

F32 = jnp.float32
BF16 = jnp.bfloat16


def _alibi_slopes(n_moba, n_dil):
    n = n_moba + n_dil
    s = 2.0 ** (-8.0 * np.arange(1, n + 1) / n)
    return jnp.asarray(s[0::2], F32), jnp.asarray(s[1::2], F32)


def _inproj_kernel(x_ref, g_ref, w_ref, b_ref, om_ref, od_ref, og_ref, h_scr, *, qscale):
    j = pl.program_id(1)

    @pl.when(j == 0)
    def _():
        x = x_ref[...]
        ms = jnp.mean(x * x, axis=-1, keepdims=True)
        h_scr[...] = (x * lax.rsqrt(ms + RMS_EPS) * g_ref[...]).astype(BF16)

    acc = jnp.dot(h_scr[...], w_ref[...], preferred_element_type=F32)

    @pl.when(j == 0)
    def _():
        om_ref[...] = (acc * qscale).astype(BF16)

    @pl.when((j > 0) & (j < 3))
    def _():
        om_ref[...] = acc.astype(BF16)

    @pl.when(j == 3)
    def _():
        od_ref[...] = acc * qscale

    @pl.when((j > 3) & (j < 6))
    def _():
        od_ref[...] = acc

    @pl.when(j >= 6)
    def _():
        og_ref[...] = jax.nn.sigmoid(acc + b_ref[...]).astype(BF16)


def _inproj(x2, g1, w_in, b_gates, width, tm):
    n, d = x2.shape
    cols = w_in.shape[1]
    tn = width
    assert cols == 6 * width + 2 * d and d % tn == 0 and n % tm == 0
    n_gate = (2 * d) // tn
    qscale = HEAD_DIM ** -0.5 * LOG2E
    return pl.pallas_call(
        functools.partial(_inproj_kernel, qscale=qscale),
        grid=(n // tm, 6 + n_gate),
        in_specs=[
            pl.BlockSpec((tm, d), lambda i, j: (i, 0)),
            pl.BlockSpec((1, d), lambda i, j: (0, 0)),
            pl.BlockSpec((d, tn), lambda i, j: (0, j)),
            pl.BlockSpec((1, tn), lambda i, j: (0, jnp.clip(j - 6, 0, n_gate - 1))),
        ],
        out_specs=[
            pl.BlockSpec((tm, tn), lambda i, j: (i, jnp.minimum(j, 2))),
            pl.BlockSpec((tm, tn), lambda i, j: (i, jnp.clip(j - 3, 0, 2))),
            pl.BlockSpec((tm, tn), lambda i, j: (i, jnp.clip(j - 6, 0, n_gate - 1))),
        ],
        out_shape=[
            jax.ShapeDtypeStruct((n, 3 * width), BF16),
            jax.ShapeDtypeStruct((n, 3 * width), F32),
            jax.ShapeDtypeStruct((n, 2 * d), BF16),
        ],
        scratch_shapes=[pltpu.VMEM((tm, d), BF16)],
        compiler_params=pltpu.CompilerParams(
            dimension_semantics=("arbitrary", "arbitrary"), vmem_limit_bytes=VMEM_LIMIT),
        name="inproj",
    )(x2, g1.reshape(1, d), w_in, b_gates.reshape(1, 2 * d))


def _nt_dot(a, b):
    return lax.dot_general(a, b, (((1,), (1,)), ((), ())), preferred_element_type=F32)


def _dot(a, b):
    return jnp.dot(a, b, preferred_element_type=F32)


MOBA_TILE = 2 * MOBA_BLOCK
ALIBI_SPLIT = 3
VT_ROWS = HEAD_DIM + 16


def _moba_kernel(slopes_ref, q_ref, k_ref, v_ref, o_ref, kaug, vt, kmh, kml, causal, sbuf, pbuf,
                 *, nb, nbp, hpg):
    blk = MOBA_BLOCK
    hd = HEAD_DIM
    tq = tk = MOBA_TILE
    hg = pl.program_id(1)
    qp = pl.program_id(2)
    heads = range(hpg)
    slope2 = [slopes_ref[hg * hpg + g] * LOG2E for g in heads]
    cols = [slice(g * hd, (g + 1) * hd) for g in heads]
    s_len = k_ref.shape[0]
    c_hi0 = nbp + ALIBI_SPLIT

    @pl.when(qp == 0)
    def _init():
        pos = lax.broadcasted_iota(jnp.int32, (s_len, LANES), 0)
        lane = lax.broadcasted_iota(jnp.int32, (s_len, LANES), 1)
        off = pos % tk
        aug = jnp.where(lane < nbp, jnp.where(pos // blk == lane, 1.0, 0.0),
                        jnp.where(lane < c_hi0, (off % blk).astype(F32),
                                  jnp.where(lane < c_hi0 + ALIBI_SPLIT,
                                            (off - off % blk).astype(F32), 0.0))).astype(BF16)
        row = lax.broadcasted_iota(jnp.int32, (VT_ROWS - hd, s_len), 0)
        ones_row = jnp.where(row == 0, 1.0, 0.0).astype(BF16)
        for g in heads:
            kaug[g, :, 0:hd] = k_ref[:, cols[g]]
            kaug[g, :, hd:hd + LANES] = aug

            def xpose(n, carry, g=g):
                r0 = pl.multiple_of(n * blk, blk)
                vt[g, 0:hd, pl.ds(r0, blk)] = (
                    v_ref[pl.ds(r0, blk), cols[g]].astype(F32).T.astype(BF16))
                return carry

            lax.fori_loop(0, nb, xpose, 0)
            vt[g, hd:VT_ROWS, :] = ones_row
            km = jnp.sum(k_ref[:, cols[g]].astype(F32).reshape(nb, blk, hd), axis=1) * (1.0 / blk)
            if nb < nbp:
                km = jnp.concatenate([km, jnp.zeros((nbp - nb, hd), F32)], axis=0)
            hi = km.astype(BF16)
            kmh[g] = hi
            kml[g] = (km - hi.astype(F32)).astype(BF16)
        cc = lax.broadcasted_iota(jnp.int32, (tk, tq), 0)
        rr = lax.broadcasted_iota(jnp.int32, (tk, tq), 1)
        causal[...] = jnp.where(cc > rr, NEG, 0.0)

    lane_q = lax.broadcasted_iota(jnp.int32, (1, tq), 1)
    qblk = qp * (tq // blk) + lane_q // blk
    t_q = qp * tq + lane_q
    bidx = lax.broadcasted_iota(jnp.int32, (nbp, tq), 0)
    ri = lax.broadcasted_iota(jnp.int32, (16, tq), 0)
    tail = jnp.zeros((LANES - nbp - 16, tq), BF16)

    def augmented_queries(g):
        qt = q_ref[:, cols[g]].astype(F32).T.astype(BF16)
        gate = _dot(kmh[g], qt) + _dot(kml[g], qt)
        gt = jnp.where(bidx < qblk, gate, -jnp.inf)
        selneg = jnp.full((nbp, tq), NEG, F32)
        for _ in range(min(MOBA_TOPK, nb)):
            mx = jnp.max(gt, axis=0, keepdims=True)
            idx = jnp.min(jnp.where(gt == mx, bidx, nbp), axis=0, keepdims=True)
            hit = bidx == idx
            selneg = jnp.where(hit & (mx > -jnp.inf), 0.0, selneg)
            gt = jnp.where(hit, -jnp.inf, gt)
        sv = jnp.full((16, tq), slope2[g], F32)
        pieces = []
        for _ in range(ALIBI_SPLIT):
            pc = sv.astype(BF16).astype(F32)
            pieces.append(pc)
            sv = sv - pc
        srows = jnp.zeros((16, tq), F32)
        for j in range(2 * ALIBI_SPLIT):
            srows = jnp.where(ri == j, pieces[j % ALIBI_SPLIT], srows)

        def augment(sel):
            return jnp.concatenate([qt, sel.astype(BF16), srows.astype(BF16), tail], axis=0)

        return augment(selneg), augment(jnp.where(bidx == qblk, 0.0, selneg))

    qaugs = [augmented_queries(g) for g in heads]

    def scores(g, k0, qa):
        return _dot(kaug[g, pl.ds(k0, tk), :], qa)

    def softmax(g, k0, raw, m):
        shift = slope2[g] * (k0 - t_q).astype(F32)
        m_new = jnp.maximum(m, jnp.max(raw, axis=0, keepdims=True) + shift)
        return m_new, jnp.exp2(raw - (m_new - shift)), jnp.exp2(m - m_new)

    def values(g, k0):
        return _dot(vt[g, :, pl.ds(k0, tk)], pbuf[g])

    for g in heads:
        sbuf[g] = scores(g, pl.multiple_of(qp * tk, tk), qaugs[g][1]) + causal[...]
        pbuf[g] = jnp.zeros((tk, tq), BF16)

    def step(j, st):
        k_cur = pl.multiple_of((qp - j) * tk, tk)
        k_prev = pl.multiple_of(jnp.minimum(k_cur + tk, qp * tk), tk)
        raws = [scores(g, k_cur - tk, qaugs[g][0]) for g in heads]
        pvs = [values(g, k_prev) for g in heads]
        out = []
        for g in heads:
            m, alpha_prev, acc = st[g]
            m_new, p, alpha = softmax(g, k_cur, sbuf[g], m)
            acc = alpha_prev * acc + pvs[g]
            pbuf[g] = p.astype(BF16)
            sbuf[g] = raws[g]
            out.append((m_new, alpha, acc))
        return tuple(out)

    init = (jnp.full((1, tq), -jnp.inf, F32), jnp.ones((1, tq), F32),
            jnp.zeros((VT_ROWS, tq), F32))
    fin = lax.fori_loop(0, qp, step, tuple(init for _ in heads))
    k_last = pl.multiple_of(jnp.minimum(tk, qp * tk), tk)
    for g in heads:
        m, alpha_prev, acc = fin[g]
        acc = alpha_prev * acc + values(g, k_last)
        _, p, alpha = softmax(g, 0, sbuf[g], m)
        pbuf[g] = p.astype(BF16)
        acc = alpha * acc + values(g, 0)
        o_ref[:, cols[g]] = (acc[0:hd, :] / acc[hd:hd + 1, :]).T.astype(o_ref.dtype)


def _moba(projm, slopes, bsz, seq, n_heads):
    width = n_heads * HEAD_DIM
    nb = seq // MOBA_BLOCK
    nbp = -(-nb // 16) * 16
    tile = MOBA_TILE
    hpg = 2 if n_heads % 2 == 0 else 1
    ng = n_heads // hpg
    gw = hpg * HEAD_DIM
    assert seq % tile == 0 and nbp + 16 <= LANES
    pm = projm.reshape(bsz, seq, 3 * width)
    grid_spec = pltpu.PrefetchScalarGridSpec(
        num_scalar_prefetch=1,
        grid=(bsz, ng, seq // tile),
        in_specs=[
            pl.BlockSpec((None, tile, gw), lambda b, h, i, s: (b, i, h)),
            pl.BlockSpec((None, seq, gw), lambda b, h, i, s: (b, 0, ng + h)),
            pl.BlockSpec((None, seq, gw), lambda b, h, i, s: (b, 0, 2 * ng + h)),
        ],
        out_specs=pl.BlockSpec((None, tile, gw), lambda b, h, i, s: (b, i, h)),
        scratch_shapes=[
            pltpu.VMEM((hpg, seq, HEAD_DIM + LANES), BF16),
            pltpu.VMEM((hpg, VT_ROWS, seq), BF16),
            pltpu.VMEM((hpg, nbp, HEAD_DIM), BF16),
            pltpu.VMEM((hpg, nbp, HEAD_DIM), BF16),
            pltpu.VMEM((tile, tile), F32),
            pltpu.VMEM((hpg, tile, tile), F32),
            pltpu.VMEM((hpg, tile, tile), BF16),
        ],
    )
    out = pl.pallas_call(
        functools.partial(_moba_kernel, nb=nb, nbp=nbp, hpg=hpg),
        grid_spec=grid_spec,
        out_shape=jax.ShapeDtypeStruct((bsz, seq, width), BF16),
        compiler_params=pltpu.CompilerParams(
            dimension_semantics=("arbitrary", "arbitrary", "arbitrary"),
            vmem_limit_bytes=VMEM_LIMIT),
        name="moba",
    )(slopes, pm, pm, pm)
    return out.reshape(bsz * seq, width)


def _dil_kernel(slopes_ref, q_ref, kc_ref, kp_ref, vc_ref, vp_ref, o_ref, bias, osc, lsc):
    w = DIL_STEPS
    chunk = q_ref.shape[0]
    h = pl.program_id(1)
    c = pl.program_id(2)
    slope2 = slopes_ref[h] * LOG2E

    @pl.when(c == 0)
    def _init():
        i = lax.broadcasted_iota(jnp.int32, (w, 2 * w), 0)
        j = lax.broadcasted_iota(jnp.int32, (w, 2 * w), 1)
        diff = i + w - j
        ok = (diff >= 0) & (diff <= w)
        for p, (_, d) in enumerate(DIL_PATTERNS):
            bias[p] = jnp.where(ok, diff.astype(F32) * (-slope2 * d), NEG)

    jj = lax.broadcasted_iota(jnp.int32, (w, 2 * w), 1)
    first_pen = jnp.where(jj < w, jnp.where(c > 0, 0.0, NEG), 0.0)

    def rows(start, d):
        return pl.ds(start, w) if d == 1 else pl.ds(start, w, stride=d)

    def tiles(p, d, qstarts, kprev_ref, vprev_ref, pstarts, from_prev_chunk):
        n = len(qstarts)
        ss = []
        for qstart, pstart in zip(qstarts, pstarts):
            q = q_ref[rows(qstart, d), :].astype(BF16)
            k2 = jnp.concatenate([kprev_ref[rows(pstart, d), :], kc_ref[rows(qstart, d), :]],
                                 axis=0).astype(BF16)
            ss.append(_nt_dot(q, k2))
        prs, ms, ls = [], [], []
        for s in ss:
            s = s + bias[p]
            if from_prev_chunk:
                s = s + first_pen
            m = jnp.max(s, axis=-1, keepdims=True)
            pr = jnp.exp2(s - m)
            prs.append(pr.astype(BF16))
            ms.append(m)
            ls.append(jnp.sum(pr, axis=-1, keepdims=True))
        for i in range(n):
            v2 = jnp.concatenate([vprev_ref[rows(pstarts[i], d), :],
                                  vc_ref[rows(qstarts[i], d), :]], axis=0).astype(BF16)
            o = jnp.dot(prs[i], v2, preferred_element_type=F32) / ls[i]
            osc[p, rows(qstarts[i], d), :] = o
            lsc[p, rows(qstarts[i], d), :] = jnp.broadcast_to(ms[i] + jnp.log2(ls[i]), (w, LANES))

    def batch_size(n):
        return max(g for g in range(1, DIL_BATCH + 1) if n % g == 0)

    for p, (_, d) in enumerate(DIL_PATTERNS):
        span = w * d
        n_sb = chunk // span

        g1 = batch_size(d)

        def first(t, carry, p=p, d=d, span=span, g1=g1):
            rs = [t * g1 + i for i in range(g1)]
            tiles(p, d, rs, kp_ref, vp_ref, [chunk - span + r for r in rs], True)
            return carry

        lax.fori_loop(0, d // g1, first, 0)

        n_rest = (n_sb - 1) * d
        if n_rest:
            g2 = batch_size(n_rest)

            def rest(t, carry, p=p, d=d, span=span, g2=g2):
                idx = [t * g2 + i for i in range(g2)]
                qstarts = [(1 + x // d) * span + x % d for x in idx]
                if d == 1:
                    qstarts = [pl.multiple_of(x, w) for x in qstarts]
                tiles(p, d, qstarts, kc_ref, vc_ref, [x - span for x in qstarts], False)
                return carry

            lax.fori_loop(0, n_rest // g2, rest, 0)

    rb = 256

    def merge(t, carry):
        r0 = pl.multiple_of(t * rb, rb)
        ls = [lsc[p, pl.ds(r0, rb), :] for p in range(len(DIL_PATTERNS))]
        mx = functools.reduce(jnp.maximum, ls)
        ws = [jnp.exp2(x - mx) for x in ls]
        num = ws[0] * osc[0, pl.ds(r0, rb), :]
        den = ws[0]
        for p in range(1, len(DIL_PATTERNS)):
            num = num + ws[p] * osc[p, pl.ds(r0, rb), :]
            den = den + ws[p]
        o_ref[pl.ds(r0, rb), :] = (num / den).astype(o_ref.dtype)
        return carry

    lax.fori_loop(0, chunk // rb, merge, 0)


def _dilated(projd, slopes, bsz, seq, n_heads):
    width = n_heads * HEAD_DIM
    chunk = DIL_CHUNK
    assert seq % chunk == 0
    assert all(win // d == DIL_STEPS and chunk % win == 0 for win, d in DIL_PATTERNS)
    pd_ = projd.reshape(bsz, seq, 3 * width)
    blk = (None, chunk, HEAD_DIM)
    npat = len(DIL_PATTERNS)
    grid_spec = pltpu.PrefetchScalarGridSpec(
        num_scalar_prefetch=1,
        grid=(bsz, n_heads, seq // chunk),
        in_specs=[
            pl.BlockSpec(blk, lambda b, h, c, s: (b, c, h)),
            pl.BlockSpec(blk, lambda b, h, c, s: (b, c, n_heads + h)),
            pl.BlockSpec(blk, lambda b, h, c, s: (b, jnp.maximum(c - 1, 0), n_heads + h)),
            pl.BlockSpec(blk, lambda b, h, c, s: (b, c, 2 * n_heads + h)),
            pl.BlockSpec(blk, lambda b, h, c, s: (b, jnp.maximum(c - 1, 0), 2 * n_heads + h)),
        ],
        out_specs=pl.BlockSpec(blk, lambda b, h, c, s: (b, c, h)),
        scratch_shapes=[
            pltpu.VMEM((npat, DIL_STEPS, 2 * DIL_STEPS), F32),
            pltpu.VMEM((npat, chunk, HEAD_DIM), F32),
            pltpu.VMEM((npat, chunk, LANES), F32),
        ],
    )
    out = pl.pallas_call(
        _dil_kernel,
        grid_spec=grid_spec,
        out_shape=jax.ShapeDtypeStruct((bsz, seq, width), BF16),
        compiler_params=pltpu.CompilerParams(
            dimension_semantics=("arbitrary", "arbitrary", "arbitrary"),
            vmem_limit_bytes=VMEM_LIMIT),
        name="dilated",
    )(slopes, pd_, pd_, pd_, pd_, pd_)
    return out.reshape(bsz * seq, width)


def _store_slabs(ref, val):
    rows, d = val.shape
    slabs = d // LANES
    for s in range(slabs):
        ref[pl.ds(s, rows, stride=slabs), :] = val[:, s * LANES:(s + 1) * LANES]


def _load_slab(ref3, slot, first_row, rows, slabs, s):
    return ref3[slot, pl.ds(first_row * slabs + s, rows, stride=slabs), :]


def _outproj_kernel(am_ref, ad_ref, gm_ref, gd_ref, x_ref, wm_ref, wd_ref, wo_ref, g2_ref,
                    wrh_ref, wrl_ref, br_ref, x1_ref, h2_ref, rt_ref, *, n_groups, per_group):
    y_m = jnp.dot(am_ref[...], wm_ref[...], preferred_element_type=F32)
    y_d = jnp.dot(ad_ref[...], wd_ref[...], preferred_element_type=F32)
    z = (gm_ref[...].astype(F32) * y_m + gd_ref[...].astype(F32) * y_d).astype(BF16)
    x1 = x_ref[...] + jnp.dot(z, wo_ref[...], preferred_element_type=F32)
    x1_ref[...] = x1
    ms = jnp.mean(x1 * x1, axis=-1, keepdims=True)
    h2 = x1 * lax.rsqrt(ms + RMS_EPS) * g2_ref[...]
    _store_slabs(h2_ref, h2)

    hi = h2.astype(BF16)
    lo = (h2 - hi.astype(F32)).astype(BF16)
    logits = (jnp.dot(hi, wrh_ref[...], preferred_element_type=F32)
              + (jnp.dot(hi, wrl_ref[...], preferred_element_type=F32)
                 + jnp.dot(lo, wrh_ref[...], preferred_element_type=F32))
              + br_ref[...])

    lane = lax.broadcasted_iota(jnp.int32, logits.shape, 1)
    big = jnp.int32(1 << 20)
    isg = lane < n_groups
    gl = jnp.where(isg, logits, -jnp.inf)
    gex = jnp.exp(gl - jnp.max(gl, axis=-1, keepdims=True))
    gprob = gex / jnp.sum(gex, axis=-1, keepdims=True)
    gtp = jnp.max(gprob, axis=-1, keepdims=True)
    gtop = jnp.min(jnp.where((gprob == gtp) & isg, lane, big), axis=-1, keepdims=True)
    e_lo = n_groups + gtop * per_group
    ise = (lane >= e_lo) & (lane < e_lo + per_group)
    el = jnp.where(ise, logits, -jnp.inf)
    eex = jnp.exp(el - jnp.max(el, axis=-1, keepdims=True))
    eprob = jnp.where(ise, eex / jnp.sum(eex, axis=-1, keepdims=True), -1.0)
    p1 = jnp.max(eprob, axis=-1, keepdims=True)
    i1 = jnp.min(jnp.where(eprob == p1, lane, big), axis=-1, keepdims=True)
    eprob2 = jnp.where(lane == i1, -1.0, eprob)
    p2 = jnp.max(eprob2, axis=-1, keepdims=True)
    i2 = jnp.min(jnp.where(eprob2 == p2, lane, big), axis=-1, keepdims=True)
    den = p1 + p2
    w1 = gtp * p1 / den
    w2 = gtp * p2 / den
    e1 = (i1 - n_groups).astype(F32)
    e2 = (i2 - n_groups).astype(F32)
    rt_ref[...] = jnp.where(lane == 0, w1, jnp.where(lane == 1, w2, jnp.where(
        lane == 2, e1, jnp.where(lane == 3, e2, 0.0))))


def _outproj(a_m, a_d, gates, x2, w_m, w_d, w_o, g2, wr_hi, wr_lo, b_r, n_groups, per_group, tm):
    n, d = x2.shape
    width = a_m.shape[1]
    assert n % tm == 0

    def const(shape):
        return pl.BlockSpec(shape, lambda i: (0, 0))

    return pl.pallas_call(
        functools.partial(_outproj_kernel, n_groups=n_groups, per_group=per_group),
        grid=(n // tm,),
        in_specs=[
            pl.BlockSpec((tm, width), lambda i: (i, 0)),
            pl.BlockSpec((tm, width), lambda i: (i, 0)),
            pl.BlockSpec((tm, d), lambda i: (i, 0)),
            pl.BlockSpec((tm, d), lambda i: (i, 1)),
            pl.BlockSpec((tm, d), lambda i: (i, 0)),
            const((width, d)), const((width, d)), const((d, d)), const((1, d)),
            const((d, LANES)), const((d, LANES)), const((1, LANES)),
        ],
        out_specs=[
            pl.BlockSpec((tm, d), lambda i: (i, 0)),
            pl.BlockSpec((tm * (d // LANES), LANES), lambda i: (i, 0)),
            pl.BlockSpec((tm, LANES), lambda i: (i, 0)),
        ],
        out_shape=[
            jax.ShapeDtypeStruct((n, d), F32),
            jax.ShapeDtypeStruct((n * (d // LANES), LANES), F32),
            jax.ShapeDtypeStruct((n, LANES), F32),
        ],
        compiler_params=pltpu.CompilerParams(
            dimension_semantics=("arbitrary",), vmem_limit_bytes=VMEM_LIMIT),
        name="outproj",
    )(a_m, a_d, gates, gates, x2, w_m, w_d, w_o, g2.reshape(1, d), wr_hi, wr_lo, b_r)


DMA_PRIORITIES = 2


def _start_row_gather(src_hbm, idx_ref, dst, sem, n_rows, slabs):
    def issue(t, carry):
        for k in range(DMA_PRIORITIES):
            r = t * DMA_PRIORITIES + k
            src_row = pl.multiple_of(idx_ref[0, 0, r] * slabs, slabs)
            pltpu.make_async_copy(src_hbm.at[pl.ds(src_row, slabs), :],
                                  dst.at[pl.ds(pl.multiple_of(r * slabs, slabs), slabs), :],
                                  sem).start(priority=k)
        return carry

    assert n_rows % DMA_PRIORITIES == 0
    lax.fori_loop(0, n_rows // DMA_PRIORITIES, issue, 0)


def _wait_row_gather(src_hbm, dst, sem, n_rows, slabs):
    pltpu.make_async_copy(src_hbm.at[pl.ds(0, n_rows * slabs), :], dst, sem).wait()


def _experts_kernel(blk_e_ref, nvalid_ref, tok_ref, tok_next_ref, h_hbm, wgu_ref, wd_ref, y_ref,
                    xbuf, sem, *, d_expert, tb):
    del blk_e_ref
    i = pl.program_id(0)
    nvalid = nvalid_ref[0]
    slabs = xbuf.shape[1] // tb
    slot = i % 2

    @pl.when((i == 0) & (i < nvalid))
    def _():
        _start_row_gather(h_hbm, tok_ref, xbuf.at[0], sem.at[0], tb, slabs)

    @pl.when(i + 1 < nvalid)
    def _():
        _start_row_gather(h_hbm, tok_next_ref, xbuf.at[1 - slot], sem.at[1 - slot], tb, slabs)

    @pl.when(i < nvalid)
    def _():
        _wait_row_gather(h_hbm, xbuf.at[slot], sem.at[slot], tb, slabs)
        x = jnp.concatenate([_load_slab(xbuf, slot, 0, tb, slabs, s).astype(BF16)
                             for s in range(slabs)], axis=1)
        gu = jnp.dot(x, wgu_ref[...], preferred_element_type=F32)
        a = (jax.nn.silu(gu[:, :d_expert]) * gu[:, d_expert:]).astype(BF16)
        _store_slabs(y_ref, jnp.dot(a, wd_ref[...], preferred_element_type=F32))

    @pl.when(i >= nvalid)
    def _():
        y_ref[...] = jnp.zeros_like(y_ref)


def _experts(h2s, tok_blocks, blk_e, nvalid, wgu, wd, tb):
    d = wgu.shape[1]
    slabs = d // LANES
    n_blk = tok_blocks.shape[0]
    d_expert = wd.shape[1]
    grid_spec = pltpu.PrefetchScalarGridSpec(
        num_scalar_prefetch=2,
        grid=(n_blk,),
        in_specs=[
            pl.BlockSpec((1, 1, tb), lambda i, be, nv: (i, 0, 0), memory_space=pltpu.SMEM),
            pl.BlockSpec((1, 1, tb), lambda i, be, nv: (jnp.minimum(i + 1, n_blk - 1), 0, 0),
                         memory_space=pltpu.SMEM),
            pl.BlockSpec(memory_space=pl.ANY),
            pl.BlockSpec((None, d, 2 * d_expert), lambda i, be, nv: (be[i], 0, 0)),
            pl.BlockSpec((None, d_expert, d), lambda i, be, nv: (be[i], 0, 0)),
        ],
        out_specs=pl.BlockSpec((tb * slabs, LANES), lambda i, be, nv: (i, 0)),
        scratch_shapes=[
            pltpu.VMEM((2, tb * slabs, LANES), F32),
            pltpu.SemaphoreType.DMA((2,)),
        ],
    )
    return pl.pallas_call(
        functools.partial(_experts_kernel, d_expert=d_expert, tb=tb),
        grid_spec=grid_spec,
        out_shape=jax.ShapeDtypeStruct((n_blk * tb * slabs, LANES), F32),
        compiler_params=pltpu.CompilerParams(
            dimension_semantics=("arbitrary",), vmem_limit_bytes=VMEM_LIMIT),
        name="experts",
    )(blk_e, nvalid, tok_blocks, tok_blocks, h2s, wgu, wd)


def _combine_kernel(pos_ref, pos_next_ref, x1_ref, rt_ref, gf_ref, y_hbm, o_ref, ybuf, sem):
    i = pl.program_id(0)
    n_steps = pl.num_programs(0)
    tm = x1_ref.shape[0]
    slot = i % 2

    slabs = x1_ref.shape[1] // LANES

    @pl.when(i == 0)
    def _():
        _start_row_gather(y_hbm, pos_ref, ybuf.at[0], sem.at[0], 2 * tm, slabs)

    @pl.when(i + 1 < n_steps)
    def _():
        _start_row_gather(y_hbm, pos_next_ref, ybuf.at[1 - slot], sem.at[1 - slot], 2 * tm, slabs)

    _wait_row_gather(y_hbm, ybuf.at[slot], sem.at[slot], 2 * tm, slabs)
    rt = rt_ref[...]
    x = jnp.concatenate(
        [x1_ref[:, s * LANES:(s + 1) * LANES]
         + (rt[:, 0:1] * _load_slab(ybuf, slot, 0, tm, slabs, s)
            + rt[:, 1:2] * _load_slab(ybuf, slot, tm, tm, slabs, s))
         for s in range(slabs)], axis=1)
    ms = jnp.mean(x * x, axis=-1, keepdims=True)
    o_ref[...] = x * lax.rsqrt(ms + RMS_EPS) * gf_ref[...]


def _combine(x1, rt, gf, y, pos_blocks, tm):
    n, d = x1.shape
    n_steps = n // tm
    return pl.pallas_call(
        _combine_kernel,
        grid=(n_steps,),
        in_specs=[
            pl.BlockSpec((1, 1, 2 * tm), lambda i: (i, 0, 0), memory_space=pltpu.SMEM),
            pl.BlockSpec((1, 1, 2 * tm), lambda i: (jnp.minimum(i + 1, n_steps - 1), 0, 0),
                         memory_space=pltpu.SMEM),
            pl.BlockSpec((tm, d), lambda i: (i, 0)),
            pl.BlockSpec((tm, LANES), lambda i: (i, 0)),
            pl.BlockSpec((1, d), lambda i: (0, 0)),
            pl.BlockSpec(memory_space=pl.ANY),
        ],
        out_specs=pl.BlockSpec((tm, d), lambda i: (i, 0)),
        out_shape=jax.ShapeDtypeStruct((n, d), F32),
        scratch_shapes=[
            pltpu.VMEM((2, 2 * tm * (d // LANES), LANES), F32),
            pltpu.SemaphoreType.DMA((2,)),
        ],
        compiler_params=pltpu.CompilerParams(
            dimension_semantics=("arbitrary",), vmem_limit_bytes=VMEM_LIMIT),
        name="combine",
    )(pos_blocks, pos_blocks, x1, rt, gf.reshape(1, d), y)


def _dispatch_tables(expert_id, n_experts, tb, tm):
    n = expert_id.shape[0]
    a = n * EXPERT_TOPK
    e_flat = expert_id.reshape(a)
    tok_flat = jnp.arange(a, dtype=jnp.int32) // EXPERT_TOPK
    order = jnp.argsort(e_flat)
    e_sorted = e_flat[order]
    counts = jnp.bincount(e_flat, length=n_experts).astype(jnp.int32)
    padded = ((counts + tb - 1) // tb) * tb
    start = jnp.cumsum(counts) - counts
    pend = jnp.cumsum(padded)
    pstart = pend - padded
    dest = pstart[e_sorted] + (jnp.arange(a, dtype=jnp.int32) - start[e_sorted])
    n_blk = (a + n_experts * tb) // tb
    blk_start = jnp.arange(n_blk, dtype=jnp.int32) * tb
    blk_e = jnp.clip(jnp.sum(pend[None, :] <= blk_start[:, None], axis=1), 0,
                     n_experts - 1).astype(jnp.int32)
    nvalid = (pend[-1] // tb).astype(jnp.int32).reshape(1)
    row = jnp.arange(n_blk * tb, dtype=jnp.int32)
    e_row = jnp.repeat(blk_e, tb)
    i_in = row - pstart[e_row]
    src = jnp.clip(start[e_row] + i_in, 0, a - 1)
    buf_tok = jnp.where((i_in >= 0) & (i_in < counts[e_row]), tok_flat[order][src], 0)
    pos = dest[jnp.argsort(order)]
    tok_blocks = buf_tok.reshape(n_blk, 1, tb)
    pos_blocks = pos.reshape(n // tm, tm, EXPERT_TOPK).transpose(0, 2, 1).reshape(
        n // tm, 1, EXPERT_TOPK * tm)
    return tok_blocks, blk_e, nvalid, pos_blocks


def _pick_tile(n, pref):
    t = pref
    while n % t:
        t //= 2
    return t


def kernel(x, norm1_g, w_in, b_gates, w_out_moba, w_out_dil, w_o, norm2_g, w_group, b_group,
           w_expert, b_expert, w_gate, w_up, w_down, norm_f_g):
    bsz, seq, d = x.shape
    n = bsz * seq
    assert w_in.shape[0] == 1, "one layer: the final RMSNorm is fused into the combine stage"
    width = w_out_moba.shape[1]
    assert w_out_dil.shape[1] == width and width % HEAD_DIM == 0
    n_heads = width // HEAD_DIM
    n_groups = w_group.shape[-1]
    n_experts = w_expert.shape[-1]
    per_group = n_experts // n_groups
    assert n_groups + n_experts <= LANES
    slopes_m, slopes_d = _alibi_slopes(n_heads, n_heads)
    tm_in = _pick_tile(n, 512)
    tm_out = _pick_tile(n, 256)
    tb = 512
    tm_c = _pick_tile(n, 256)
    l = 0

    x2 = x.reshape(n, d)
    projm, projd, gates = _inproj(x2, norm1_g[l], w_in[l].astype(BF16), b_gates[l], width, tm_in)
    o_m = _moba(projm, slopes_m, bsz, seq, n_heads)
    o_d = _dilated(projd, slopes_d, bsz, seq, n_heads)

    w_r = jnp.concatenate([w_group[l], w_expert[l]], axis=1)
    w_r = jnp.pad(w_r, ((0, 0), (0, LANES - w_r.shape[1])))
    wr_hi = w_r.astype(BF16)
    wr_lo = (w_r - wr_hi.astype(F32)).astype(BF16)
    b_r = jnp.pad(jnp.concatenate([b_group[l], b_expert[l]]),
                  (0, LANES - n_groups - n_experts)).reshape(1, LANES)
    x1, h2, rt = _outproj(o_m, o_d, gates, x2, w_out_moba[l].astype(BF16),
                          w_out_dil[l].astype(BF16), w_o[l].astype(BF16), norm2_g[l],
                          wr_hi, wr_lo, b_r, n_groups, per_group, tm_out)

    expert_id = rt[:, 2:2 + EXPERT_TOPK].astype(jnp.int32)
    tok_blocks, blk_e, nvalid, pos_blocks = _dispatch_tables(expert_id, n_experts, tb, tm_c)
    wgu = jnp.concatenate([w_gate[l], w_up[l]], axis=-1).astype(BF16)
    y = _experts(h2, tok_blocks, blk_e, nvalid, wgu, w_down[l].astype(BF16), tb)
    out = _combine(x1, rt, norm_f_g, y, pos_blocks, tm_c)
    return out.reshape(bsz, seq, d)
```

```python
import functools

import numpy as np
import jax
import jax.numpy as jnp
from jax import lax
from jax.experimental import pallas as pl
from jax.experimental.pallas import tpu as pltpu

HEAD_DIM = 128
MOBA_BLOCK = 256
MOBA_TOPK = 3
DIL_PATTERNS = ((128, 1), (512, 4), (2048, 16))
DIL_STEPS = 128
DIL_CHUNK = 2048
DIL_BATCH = 16
EXPERT_TOPK = 2
RMS_EPS = 1e-6
NEG = -1e30
LOG2E = 1.4426950408889634
LANES = 128
VMEM_LIMIT = 56 * 1024 * 1024

F32 = jnp.float32
BF16 = jnp.bfloat16


def _alibi_slopes(n_moba, n_dil):
    n = n_moba + n_dil
    s = 2.0 ** (-8.0 * np.arange(1, n + 1) / n)
    return jnp.asarray(s[0::2], F32), jnp.asarray(s[1::2], F32)


def _inproj_kernel(x_ref, g_ref, w_ref, b_ref, om_ref, od_ref, og_ref, h_scr, *, qscale):
    j = pl.program_id(1)

    @pl.when(j == 0)
    def _():
        x = x_ref[...]
        ms = jnp.mean(x * x, axis=-1, keepdims=True)
        h_scr[...] = (x * lax.rsqrt(ms + RMS_EPS) * g_ref[...]).astype(BF16)

    acc = jnp.dot(h_scr[...], w_ref[...], preferred_element_type=F32)

    @pl.when(j == 0)
    def _():
        om_ref[...] = (acc * qscale).astype(BF16)

    @pl.when((j > 0) & (j < 3))
    def _():
        om_ref[...] = acc.astype(BF16)

    @pl.when(j == 3)
    def _():
        od_ref[...] = acc * qscale

    @pl.when((j > 3) & (j < 6))
    def _():
        od_ref[...] = acc

    @pl.when(j >= 6)
    def _():
        og_ref[...] = jax.nn.sigmoid(acc + b_ref[...]).astype(BF16)


def _inproj(x2, g1, w_in, b_gates, width, tm):
    n, d = x2.shape
    cols = w_in.shape[1]
    tn = width
    assert cols == 6 * width + 2 * d and d % tn == 0 and n % tm == 0
    n_gate = (2 * d) // tn
    qscale = HEAD_DIM ** -0.5 * LOG2E
    return pl.pallas_call(
        functools.partial(_inproj_kernel, qscale=qscale),
        grid=(n // tm, 6 + n_gate),
        in_specs=[
            pl.BlockSpec((tm, d), lambda i, j: (i, 0)),
            pl.BlockSpec((1, d), lambda i, j: (0, 0)),
            pl.BlockSpec((d, tn), lambda i, j: (0, j)),
            pl.BlockSpec((1, tn), lambda i, j: (0, jnp.clip(j - 6, 0, n_gate - 1))),
        ],
        out_specs=[
            pl.BlockSpec((tm, tn), lambda i, j: (i, jnp.minimum(j, 2))),
            pl.BlockSpec((tm, tn), lambda i, j: (i, jnp.clip(j - 3, 0, 2))),
            pl.BlockSpec((tm, tn), lambda i, j: (i, jnp.clip(j - 6, 0, n_gate - 1))),
        ],
        out_shape=[
            jax.ShapeDtypeStruct((n, 3 * width), BF16),
            jax.ShapeDtypeStruct((n, 3 * width), F32),
            jax.ShapeDtypeStruct((n, 2 * d), BF16),
        ],
        scratch_shapes=[pltpu.VMEM((tm, d), BF16)],
        compiler_params=pltpu.CompilerParams(
            dimension_semantics=("arbitrary", "arbitrary"), vmem_limit_bytes=VMEM_LIMIT),
        name="inproj",
    )(x2, g1.reshape(1, d), w_in, b_gates.reshape(1, 2 * d))


def _nt_dot(a, b):
    return lax.dot_general(a, b, (((1,), (1,)), ((), ())), preferred_element_type=F32)


def _dot(a, b):
    return jnp.dot(a, b, preferred_element_type=F32)


MOBA_TILE = 2 * MOBA_BLOCK
ALIBI_SPLIT = 3
VT_ROWS = HEAD_DIM + 16


def _moba_kernel(slopes_ref, q_ref, k_ref, v_ref, o_ref, kaug, vt, kmh, kml, causal, sbuf, pbuf,
                 *, nb, nbp, hpg):
    blk = MOBA_BLOCK
    hd = HEAD_DIM
    tq = tk = MOBA_TILE
    hg = pl.program_id(1)
    qp = pl.program_id(2)
    heads = range(hpg)
    slope2 = [slopes_ref[hg * hpg + g] * LOG2E for g in heads]
    cols = [slice(g * hd, (g + 1) * hd) for g in heads]
    s_len = k_ref.shape[0]
    c_hi0 = nbp + ALIBI_SPLIT

    @pl.when(qp == 0)
    def _init():
        pos = lax.broadcasted_iota(jnp.int32, (s_len, LANES), 0)
        lane = lax.broadcasted_iota(jnp.int32, (s_len, LANES), 1)
        off = pos % tk
        aug = jnp.where(lane < nbp, jnp.where(pos // blk == lane, 1.0, 0.0),
                        jnp.where(lane < c_hi0, (off % blk).astype(F32),
                                  jnp.where(lane < c_hi0 + ALIBI_SPLIT,
                                            (off - off % blk).astype(F32), 0.0))).astype(BF16)
        row = lax.broadcasted_iota(jnp.int32, (VT_ROWS - hd, s_len), 0)
        ones_row = jnp.where(row == 0, 1.0, 0.0).astype(BF16)
        for g in heads:
            kaug[g, :, 0:hd] = k_ref[:, cols[g]]
            kaug[g, :, hd:hd + LANES] = aug

            def xpose(n, carry, g=g):
                r0 = pl.multiple_of(n * blk, blk)
                vt[g, 0:hd, pl.ds(r0, blk)] = (
                    v_ref[pl.ds(r0, blk), cols[g]].astype(F32).T.astype(BF16))
                return carry

            lax.fori_loop(0, nb, xpose, 0)
            vt[g, hd:VT_ROWS, :] = ones_row
            km = jnp.sum(k_ref[:, cols[g]].astype(F32).reshape(nb, blk, hd), axis=1) * (1.0 / blk)
            if nb < nbp:
                km = jnp.concatenate([km, jnp.zeros((nbp - nb, hd), F32)], axis=0)
            hi = km.astype(BF16)
            kmh[g] = hi
            kml[g] = (km - hi.astype(F32)).astype(BF16)
        cc = lax.broadcasted_iota(jnp.int32, (tk, tq), 0)
        rr = lax.broadcasted_iota(jnp.int32, (tk, tq), 1)
        causal[...] = jnp.where(cc > rr, NEG, 0.0)

    lane_q = lax.broadcasted_iota(jnp.int32, (1, tq), 1)
    qblk = qp * (tq // blk) + lane_q // blk
    t_q = qp * tq + lane_q
    bidx = lax.broadcasted_iota(jnp.int32, (nbp, tq), 0)
    ri = lax.broadcasted_iota(jnp.int32, (16, tq), 0)
    tail = jnp.zeros((LANES - nbp - 16, tq), BF16)

    def augmented_queries(g):
        qt = q_ref[:, cols[g]].astype(F32).T.astype(BF16)
        gate = _dot(kmh[g], qt) + _dot(kml[g], qt)
        gt = jnp.where(bidx < qblk, gate, -jnp.inf)
        selneg = jnp.full((nbp, tq), NEG, F32)
        for _ in range(min(MOBA_TOPK, nb)):
            mx = jnp.max(gt, axis=0, keepdims=True)
            idx = jnp.min(jnp.where(gt == mx, bidx, nbp), axis=0, keepdims=True)
            hit = bidx == idx
            selneg = jnp.where(hit & (mx > -jnp.inf), 0.0, selneg)
            gt = jnp.where(hit, -jnp.inf, gt)
        sv = jnp.full((16, tq), slope2[g], F32)
        pieces = []
        for _ in range(ALIBI_SPLIT):
            pc = sv.astype(BF16).astype(F32)
            pieces.append(pc)
            sv = sv - pc
        srows = jnp.zeros((16, tq), F32)
        for j in range(2 * ALIBI_SPLIT):
            srows = jnp.where(ri == j, pieces[j % ALIBI_SPLIT], srows)

        def augment(sel):
            return jnp.concatenate([qt, sel.astype(BF16), srows.astype(BF16), tail], axis=0)

        return augment(selneg), augment(jnp.where(bidx == qblk, 0.0, selneg))

    qaugs = [augmented_queries(g) for g in heads]

    def scores(g, k0, qa):
        return _dot(kaug[g, pl.ds(k0, tk), :], qa)

    def softmax(g, k0, raw, m):
        shift = slope2[g] * (k0 - t_q).astype(F32)
        m_new = jnp.maximum(m, jnp.max(raw, axis=0, keepdims=True) + shift)
        return m_new, jnp.exp2(raw - (m_new - shift)), jnp.exp2(m - m_new)

    def values(g, k0):
        return _dot(vt[g, :, pl.ds(k0, tk)], pbuf[g])

    for g in heads:
        sbuf[g] = scores(g, pl.multiple_of(qp * tk, tk), qaugs[g][1]) + causal[...]
        pbuf[g] = jnp.zeros((tk, tq), BF16)

    def step(j, st):
        k_cur = pl.multiple_of((qp - j) * tk, tk)
        k_prev = pl.multiple_of(jnp.minimum(k_cur + tk, qp * tk), tk)
        raws = [scores(g, k_cur - tk, qaugs[g][0]) for g in heads]
        pvs = [values(g, k_prev) for g in heads]
        out = []
        for g in heads:
            m, alpha_prev, acc = st[g]
            m_new, p, alpha = softmax(g, k_cur, sbuf[g], m)
            acc = alpha_prev * acc + pvs[g]
            pbuf[g] = p.astype(BF16)
            sbuf[g] = raws[g]
            out.append((m_new, alpha, acc))
        return tuple(out)

    init = (jnp.full((1, tq), -jnp.inf, F32), jnp.ones((1, tq), F32),
            jnp.zeros((VT_ROWS, tq), F32))
    fin = lax.fori_loop(0, qp, step, tuple(init for _ in heads))
    k_last = pl.multiple_of(jnp.minimum(tk, qp * tk), tk)
    for g in heads:
        m, alpha_prev, acc = fin[g]
        acc = alpha_prev * acc + values(g, k_last)
        _, p, alpha = softmax(g, 0, sbuf[g], m)
        pbuf[g] = p.astype(BF16)
        acc = alpha * acc + values(g, 0)
        o_ref[:, cols[g]] = (acc[0:hd, :] / acc[hd:hd + 1, :]).T.astype(o_ref.dtype)


def _moba(projm, slopes, bsz, seq, n_heads):
    width = n_heads * HEAD_DIM
    nb = seq // MOBA_BLOCK
    nbp = -(-nb // 16) * 16
    tile = MOBA_TILE
    hpg = 2 if n_heads % 2 == 0 else 1
    ng = n_heads // hpg
    gw = hpg * HEAD_DIM
    assert seq % tile == 0 and nbp + 16 <= LANES
    pm = projm.reshape(bsz, seq, 3 * width)
    grid_spec = pltpu.PrefetchScalarGridSpec(
        num_scalar_prefetch=1,
        grid=(bsz, ng, seq // tile),
        in_specs=[
            pl.BlockSpec((None, tile, gw), lambda b, h, i, s: (b, i, h)),
            pl.BlockSpec((None, seq, gw), lambda b, h, i, s: (b, 0, ng + h)),
            pl.BlockSpec((None, seq, gw), lambda b, h, i, s: (b, 0, 2 * ng + h)),
        ],
        out_specs=pl.BlockSpec((None, tile, gw), lambda b, h, i, s: (b, i, h)),
        scratch_shapes=[
            pltpu.VMEM((hpg, seq, HEAD_DIM + LANES), BF16),
            pltpu.VMEM((hpg, VT_ROWS, seq), BF16),
            pltpu.VMEM((hpg, nbp, HEAD_DIM), BF16),
            pltpu.VMEM((hpg, nbp, HEAD_DIM), BF16),
            pltpu.VMEM((tile, tile), F32),
            pltpu.VMEM((hpg, tile, tile), F32),
            pltpu.VMEM((hpg, tile, tile), BF16),
        ],
    )
    out = pl.pallas_call(
        functools.partial(_moba_kernel, nb=nb, nbp=nbp, hpg=hpg),
        grid_spec=grid_spec,
        out_shape=jax.ShapeDtypeStruct((bsz, seq, width), BF16),
        compiler_params=pltpu.CompilerParams(
            dimension_semantics=("arbitrary", "arbitrary", "arbitrary"),
            vmem_limit_bytes=VMEM_LIMIT),
        name="moba",
    )(slopes, pm, pm, pm)
    return out.reshape(bsz * seq, width)


def _dil_kernel(slopes_ref, q_ref, kc_ref, kp_ref, vc_ref, vp_ref, o_ref, bias, osc, lsc):
    w = DIL_STEPS
    chunk = q_ref.shape[0]
    h = pl.program_id(1)
    c = pl.program_id(2)
    slope2 = slopes_ref[h] * LOG2E

    @pl.when(c == 0)
    def _init():
        i = lax.broadcasted_iota(jnp.int32, (w, 2 * w), 0)
        j = lax.broadcasted_iota(jnp.int32, (w, 2 * w), 1)
        diff = i + w - j
        ok = (diff >= 0) & (diff <= w)
        for p, (_, d) in enumerate(DIL_PATTERNS):
            bias[p] = jnp.where(ok, diff.astype(F32) * (-slope2 * d), NEG)

    jj = lax.broadcasted_iota(jnp.int32, (w, 2 * w), 1)
    first_pen = jnp.where(jj < w, jnp.where(c > 0, 0.0, NEG), 0.0)

    def rows(start, d):
        return pl.ds(start, w) if d == 1 else pl.ds(start, w, stride=d)

    def tiles(p, d, qstarts, kprev_ref, vprev_ref, pstarts, from_prev_chunk):
        n = len(qstarts)
        ss = []
        for qstart, pstart in zip(qstarts, pstarts):
            q = q_ref[rows(qstart, d), :].astype(BF16)
            k2 = jnp.concatenate([kprev_ref[rows(pstart, d), :], kc_ref[rows(qstart, d), :]],
                                 axis=0).astype(BF16)
            ss.append(_nt_dot(q, k2))
        prs, ms, ls = [], [], []
        for s in ss:
            s = s + bias[p]
            if from_prev_chunk:
                s = s + first_pen
            m = jnp.max(s, axis=-1, keepdims=True)
            pr = jnp.exp2(s - m)
            prs.append(pr.astype(BF16))
            ms.append(m)
            ls.append(jnp.sum(pr, axis=-1, keepdims=True))
        for i in range(n):
            v2 = jnp.concatenate([vprev_ref[rows(pstarts[i], d), :],
                                  vc_ref[rows(qstarts[i], d), :]], axis=0).astype(BF16)
            o = jnp.dot(prs[i], v2, preferred_element_type=F32) / ls[i]
            osc[p, rows(qstarts[i], d), :] = o
            lsc[p, rows(qstarts[i], d), :] = jnp.broadcast_to(ms[i] + jnp.log2(ls[i]), (w, LANES))

    def batch_size(n):
        return max(g for g in range(1, DIL_BATCH + 1) if n % g == 0)

    for p, (_, d) in enumerate(DIL_PATTERNS):
        span = w * d
        n_sb = chunk // span

        g1 = batch_size(d)

        def first(t, carry, p=p, d=d, span=span, g1=g1):
            rs = [t * g1 + i for i in range(g1)]
            tiles(p, d, rs, kp_ref, vp_ref, [chunk - span + r for r in rs], True)
            return carry

        lax.fori_loop(0, d // g1, first, 0)

        n_rest = (n_sb - 1) * d
        if n_rest:
            g2 = batch_size(n_rest)

            def rest(t, carry, p=p, d=d, span=span, g2=g2):
                idx = [t * g2 + i for i in range(g2)]
                qstarts = [(1 + x // d) * span + x % d for x in idx]
                if d == 1:
                    qstarts = [pl.multiple_of(x, w) for x in qstarts]
                tiles(p, d, qstarts, kc_ref, vc_ref, [x - span for x in qstarts], False)
                return carry

            lax.fori_loop(0, n_rest // g2, rest, 0)

    rb = 256

    def merge(t, carry):
        r0 = pl.multiple_of(t * rb, rb)
        ls = [lsc[p, pl.ds(r0, rb), :] for p in range(len(DIL_PATTERNS))]
        mx = functools.reduce(jnp.maximum, ls)
        ws = [jnp.exp2(x - mx) for x in ls]
        num = ws[0] * osc[0, pl.ds(r0, rb), :]
        den = ws[0]
        for p in range(1, len(DIL_PATTERNS)):
            num = num + ws[p] * osc[p, pl.ds(r0, rb), :]
            den = den + ws[p]
        o_ref[pl.ds(r0, rb), :] = (num / den).astype(o_ref.dtype)
        return carry

    lax.fori_loop(0, chunk // rb, merge, 0)


def _dilated(projd, slopes, bsz, seq, n_heads):
    width = n_heads * HEAD_DIM
    chunk = DIL_CHUNK
    assert seq % chunk == 0
    assert all(win // d == DIL_STEPS and chunk % win == 0 for win, d in DIL_PATTERNS)
    pd_ = projd.reshape(bsz, seq, 3 * width)
    blk = (None, chunk, HEAD_DIM)
    npat = len(DIL_PATTERNS)
    grid_spec = pltpu.PrefetchScalarGridSpec(
        num_scalar_prefetch=1,
        grid=(bsz, n_heads, seq // chunk),
        in_specs=[
            pl.BlockSpec(blk, lambda b, h, c, s: (b, c, h)),
            pl.BlockSpec(blk, lambda b, h, c, s: (b, c, n_heads + h)),
            pl.BlockSpec(blk, lambda b, h, c, s: (b, jnp.maximum(c - 1, 0), n_heads + h)),
            pl.BlockSpec(blk, lambda b, h, c, s: (b, c, 2 * n_heads + h)),
            pl.BlockSpec(blk, lambda b, h, c, s: (b, jnp.maximum(c - 1, 0), 2 * n_heads + h)),
        ],
        out_specs=pl.BlockSpec(blk, lambda b, h, c, s: (b, c, h)),
        scratch_shapes=[
            pltpu.VMEM((npat, DIL_STEPS, 2 * DIL_STEPS), F32),
            pltpu.VMEM((npat, chunk, HEAD_DIM), F32),
            pltpu.VMEM((npat, chunk, LANES), F32),
        ],
    )
    out = pl.pallas_call(
        _dil_kernel,
        grid_spec=grid_spec,
        out_shape=jax.ShapeDtypeStruct((bsz, seq, width), BF16),
        compiler_params=pltpu.CompilerParams(
            dimension_semantics=("arbitrary", "arbitrary", "arbitrary"),
            vmem_limit_bytes=VMEM_LIMIT),
        name="dilated",
    )(slopes, pd_, pd_, pd_, pd_, pd_)
    return out.reshape(bsz * seq, width)


def _store_slabs(ref, val):
    rows, d = val.shape
    slabs = d // LANES
    for s in range(slabs):
        ref[pl.ds(s, rows, stride=slabs), :] = val[:, s * LANES:(s + 1) * LANES]


def _load_slab(ref3, slot, first_row, rows, slabs, s):
    return ref3[slot, pl.ds(first_row * slabs + s, rows, stride=slabs), :]


def _outproj_kernel(am_ref, ad_ref, gm_ref, gd_ref, x_ref, wm_ref, wd_ref, wo_ref, g2_ref,
                    wrh_ref, wrl_ref, br_ref, x1_ref, h2_ref, rt_ref, *, n_groups, per_group):
    y_m = jnp.dot(am_ref[...], wm_ref[...], preferred_element_type=F32)
    y_d = jnp.dot(ad_ref[...], wd_ref[...], preferred_element_type=F32)
    z = (gm_ref[...].astype(F32) * y_m + gd_ref[...].astype(F32) * y_d).astype(BF16)
    x1 = x_ref[...] + jnp.dot(z, wo_ref[...], preferred_element_type=F32)
    x1_ref[...] = x1
    ms = jnp.mean(x1 * x1, axis=-1, keepdims=True)
    h2 = x1 * lax.rsqrt(ms + RMS_EPS) * g2_ref[...]
    _store_slabs(h2_ref, h2)

    hi = h2.astype(BF16)
    lo = (h2 - hi.astype(F32)).astype(BF16)
    logits = (jnp.dot(hi, wrh_ref[...], preferred_element_type=F32)
              + (jnp.dot(hi, wrl_ref[...], preferred_element_type=F32)
                 + jnp.dot(lo, wrh_ref[...], preferred_element_type=F32))
              + br_ref[...])

    lane = lax.broadcasted_iota(jnp.int32, logits.shape, 1)
    big = jnp.int32(1 << 20)
    isg = lane < n_groups
    gl = jnp.where(isg, logits, -jnp.inf)
    gex = jnp.exp(gl - jnp.max(gl, axis=-1, keepdims=True))
    gprob = gex / jnp.sum(gex, axis=-1, keepdims=True)
    gtp = jnp.max(gprob, axis=-1, keepdims=True)
    gtop = jnp.min(jnp.where((gprob == gtp) & isg, lane, big), axis=-1, keepdims=True)
    e_lo = n_groups + gtop * per_group
    ise = (lane >= e_lo) & (lane < e_lo + per_group)
    el = jnp.where(ise, logits, -jnp.inf)
    eex = jnp.exp(el - jnp.max(el, axis=-1, keepdims=True))
    eprob = jnp.where(ise, eex / jnp.sum(eex, axis=-1, keepdims=True), -1.0)
    p1 = jnp.max(eprob, axis=-1, keepdims=True)
    i1 = jnp.min(jnp.where(eprob == p1, lane, big), axis=-1, keepdims=True)
    eprob2 = jnp.where(lane == i1, -1.0, eprob)
    p2 = jnp.max(eprob2, axis=-1, keepdims=True)
    i2 = jnp.min(jnp.where(eprob2 == p2, lane, big), axis=-1, keepdims=True)
    den = p1 + p2
    w1 = gtp * p1 / den
    w2 = gtp * p2 / den
    e1 = (i1 - n_groups).astype(F32)
    e2 = (i2 - n_groups).astype(F32)
    rt_ref[...] = jnp.where(lane == 0, w1, jnp.where(lane == 1, w2, jnp.where(
        lane == 2, e1, jnp.where(lane == 3, e2, 0.0))))


def _outproj(a_m, a_d, gates, x2, w_m, w_d, w_o, g2, wr_hi, wr_lo, b_r, n_groups, per_group, tm):
    n, d = x2.shape
    width = a_m.shape[1]
    assert n % tm == 0

    def const(shape):
        return pl.BlockSpec(shape, lambda i: (0, 0))

    return pl.pallas_call(
        functools.partial(_outproj_kernel, n_groups=n_groups, per_group=per_group),
        grid=(n // tm,),
        in_specs=[
            pl.BlockSpec((tm, width), lambda i: (i, 0)),
            pl.BlockSpec((tm, width), lambda i: (i, 0)),
            pl.BlockSpec((tm, d), lambda i: (i, 0)),
            pl.BlockSpec((tm, d), lambda i: (i, 1)),
            pl.BlockSpec((tm, d), lambda i: (i, 0)),
            const((width, d)), const((width, d)), const((d, d)), const((1, d)),
            const((d, LANES)), const((d, LANES)), const((1, LANES)),
        ],
        out_specs=[
            pl.BlockSpec((tm, d), lambda i: (i, 0)),
            pl.BlockSpec((tm * (d // LANES), LANES), lambda i: (i, 0)),
            pl.BlockSpec((tm, LANES), lambda i: (i, 0)),
        ],
        out_shape=[
            jax.ShapeDtypeStruct((n, d), F32),
            jax.ShapeDtypeStruct((n * (d // LANES), LANES), F32),
            jax.ShapeDtypeStruct((n, LANES), F32),
        ],
        compiler_params=pltpu.CompilerParams(
            dimension_semantics=("arbitrary",), vmem_limit_bytes=VMEM_LIMIT),
        name="outproj",
    )(a_m, a_d, gates, gates, x2, w_m, w_d, w_o, g2.reshape(1, d), wr_hi, wr_lo, b_r)


DMA_PRIORITIES = 2


def _start_row_gather(src_hbm, idx_ref, dst, sem, n_rows, slabs):
    def issue(t, carry):
        for k in range(DMA_PRIORITIES):
            r = t * DMA_PRIORITIES + k
            src_row = pl.multiple_of(idx_ref[0, 0, r] * slabs, slabs)
            pltpu.make_async_copy(src_hbm.at[pl.ds(src_row, slabs), :],
                                  dst.at[pl.ds(pl.multiple_of(r * slabs, slabs), slabs), :],
                                  sem).start(priority=k)
        return carry

    assert n_rows % DMA_PRIORITIES == 0
    lax.fori_loop(0, n_rows // DMA_PRIORITIES, issue, 0)


def _wait_row_gather(src_hbm, dst, sem, n_rows, slabs):
    pltpu.make_async_copy(src_hbm.at[pl.ds(0, n_rows * slabs), :], dst, sem).wait()


def _experts_kernel(blk_e_ref, nvalid_ref, tok_ref, tok_next_ref, h_hbm, wgu_ref, wd_ref, y_ref,
                    xbuf, sem, *, d_expert, tb):
    del blk_e_ref
    i = pl.program_id(0)
    nvalid = nvalid_ref[0]
    slabs = xbuf.shape[1] // tb
    slot = i % 2

    @pl.when((i == 0) & (i < nvalid))
    def _():
        _start_row_gather(h_hbm, tok_ref, xbuf.at[0], sem.at[0], tb, slabs)

    @pl.when(i + 1 < nvalid)
    def _():
        _start_row_gather(h_hbm, tok_next_ref, xbuf.at[1 - slot], sem.at[1 - slot], tb, slabs)

    @pl.when(i < nvalid)
    def _():
        _wait_row_gather(h_hbm, xbuf.at[slot], sem.at[slot], tb, slabs)
        x = jnp.concatenate([_load_slab(xbuf, slot, 0, tb, slabs, s).astype(BF16)
                             for s in range(slabs)], axis=1)
        gu = jnp.dot(x, wgu_ref[...], preferred_element_type=F32)
        a = (jax.nn.silu(gu[:, :d_expert]) * gu[:, d_expert:]).astype(BF16)
        _store_slabs(y_ref, jnp.dot(a, wd_ref[...], preferred_element_type=F32))

    @pl.when(i >= nvalid)
    def _():
        y_ref[...] = jnp.zeros_like(y_ref)


def _experts(h2s, tok_blocks, blk_e, nvalid, wgu, wd, tb):
    d = wgu.shape[1]
    slabs = d // LANES
    n_blk = tok_blocks.shape[0]
    d_expert = wd.shape[1]
    grid_spec = pltpu.PrefetchScalarGridSpec(
        num_scalar_prefetch=2,
        grid=(n_blk,),
        in_specs=[
            pl.BlockSpec((1, 1, tb), lambda i, be, nv: (i, 0, 0), memory_space=pltpu.SMEM),
            pl.BlockSpec((1, 1, tb), lambda i, be, nv: (jnp.minimum(i + 1, n_blk - 1), 0, 0),
                         memory_space=pltpu.SMEM),
            pl.BlockSpec(memory_space=pl.ANY),
            pl.BlockSpec((None, d, 2 * d_expert), lambda i, be, nv: (be[i], 0, 0)),
            pl.BlockSpec((None, d_expert, d), lambda i, be, nv: (be[i], 0, 0)),
        ],
        out_specs=pl.BlockSpec((tb * slabs, LANES), lambda i, be, nv: (i, 0)),
        scratch_shapes=[
            pltpu.VMEM((2, tb * slabs, LANES), F32),
            pltpu.SemaphoreType.DMA((2,)),
        ],
    )
    return pl.pallas_call(
        functools.partial(_experts_kernel, d_expert=d_expert, tb=tb),
        grid_spec=grid_spec,
        out_shape=jax.ShapeDtypeStruct((n_blk * tb * slabs, LANES), F32),
        compiler_params=pltpu.CompilerParams(
            dimension_semantics=("arbitrary",), vmem_limit_bytes=VMEM_LIMIT),
        name="experts",
    )(blk_e, nvalid, tok_blocks, tok_blocks, h2s, wgu, wd)


def _combine_kernel(pos_ref, pos_next_ref, x1_ref, rt_ref, gf_ref, y_hbm, o_ref, ybuf, sem):
    i = pl.program_id(0)
    n_steps = pl.num_programs(0)
    tm = x1_ref.shape[0]
    slot = i % 2

    slabs = x1_ref.shape[1] // LANES

    @pl.when(i == 0)
    def _():
        _start_row_gather(y_hbm, pos_ref, ybuf.at[0], sem.at[0], 2 * tm, slabs)

    @pl.when(i + 1 < n_steps)
    def _():
        _start_row_gather(y_hbm, pos_next_ref, ybuf.at[1 - slot], sem.at[1 - slot], 2 * tm, slabs)

    _wait_row_gather(y_hbm, ybuf.at[slot], sem.at[slot], 2 * tm, slabs)
    rt = rt_ref[...]
    x = jnp.concatenate(
        [x1_ref[:, s * LANES:(s + 1) * LANES]
         + (rt[:, 0:1] * _load_slab(ybuf, slot, 0, tm, slabs, s)
            + rt[:, 1:2] * _load_slab(ybuf, slot, tm, tm, slabs, s))
         for s in range(slabs)], axis=1)
    ms = jnp.mean(x * x, axis=-1, keepdims=True)
    o_ref[...] = x * lax.rsqrt(ms + RMS_EPS) * gf_ref[...]


def _combine(x1, rt, gf, y, pos_blocks, tm):
    n, d = x1.shape
    n_steps = n // tm
    return pl.pallas_call(
        _combine_kernel,
        grid=(n_steps,),
        in_specs=[
            pl.BlockSpec((1, 1, 2 * tm), lambda i: (i, 0, 0), memory_space=pltpu.SMEM),
            pl.BlockSpec((1, 1, 2 * tm), lambda i: (jnp.minimum(i + 1, n_steps - 1), 0, 0),
                         memory_space=pltpu.SMEM),
            pl.BlockSpec((tm, d), lambda i: (i, 0)),
            pl.BlockSpec((tm, LANES), lambda i: (i, 0)),
            pl.BlockSpec((1, d), lambda i: (0, 0)),
            pl.BlockSpec(memory_space=pl.ANY),
        ],
        out_specs=pl.BlockSpec((tm, d), lambda i: (i, 0)),
        out_shape=jax.ShapeDtypeStruct((n, d), F32),
        scratch_shapes=[
            pltpu.VMEM((2, 2 * tm * (d // LANES), LANES), F32),
            pltpu.SemaphoreType.DMA((2,)),
        ],
        compiler_params=pltpu.CompilerParams(
            dimension_semantics=("arbitrary",), vmem_limit_bytes=VMEM_LIMIT),
        name="combine",
    )(pos_blocks, pos_blocks, x1, rt, gf.reshape(1, d), y)


def _dispatch_tables(expert_id, n_experts, tb, tm):
    n = expert_id.shape[0]
    a = n * EXPERT_TOPK
    e_flat = expert_id.reshape(a)
    tok_flat = jnp.arange(a, dtype=jnp.int32) // EXPERT_TOPK
    order = jnp.argsort(e_flat)
    e_sorted = e_flat[order]
    counts = jnp.bincount(e_flat, length=n_experts).astype(jnp.int32)
    padded = ((counts + tb - 1) // tb) * tb
    start = jnp.cumsum(counts) - counts
    pend = jnp.cumsum(padded)
    pstart = pend - padded
    dest = pstart[e_sorted] + (jnp.arange(a, dtype=jnp.int32) - start[e_sorted])
    n_blk = (a + n_experts * tb) // tb
    blk_start = jnp.arange(n_blk, dtype=jnp.int32) * tb
    blk_e = jnp.clip(jnp.sum(pend[None, :] <= blk_start[:, None], axis=1), 0,
                     n_experts - 1).astype(jnp.int32)
    nvalid = (pend[-1] // tb).astype(jnp.int32).reshape(1)
    row = jnp.arange(n_blk * tb, dtype=jnp.int32)
    e_row = jnp.repeat(blk_e, tb)
    i_in = row - pstart[e_row]
    src = jnp.clip(start[e_row] + i_in, 0, a - 1)
    buf_tok = jnp.where((i_in >= 0) & (i_in < counts[e_row]), tok_flat[order][src], 0)
    pos = dest[jnp.argsort(order)]
    tok_blocks = buf_tok.reshape(n_blk, 1, tb)
    pos_blocks = pos.reshape(n // tm, tm, EXPERT_TOPK).transpose(0, 2, 1).reshape(
        n // tm, 1, EXPERT_TOPK * tm)
    return tok_blocks, blk_e, nvalid, pos_blocks


def _pick_tile(n, pref):
    t = pref
    while n % t:
        t //= 2
    return t


def kernel(x, norm1_g, w_in, b_gates, w_out_moba, w_out_dil, w_o, norm2_g, w_group, b_group,
           w_expert, b_expert, w_gate, w_up, w_down, norm_f_g):
    bsz, seq, d = x.shape
    n = bsz * seq
    assert w_in.shape[0] == 1, "one layer: the final RMSNorm is fused into the combine stage"
    width = w_out_moba.shape[1]
    assert w_out_dil.shape[1] == width and width % HEAD_DIM == 0
    n_heads = width // HEAD_DIM
    n_groups = w_group.shape[-1]
    n_experts = w_expert.shape[-1]
    per_group = n_experts // n_groups
    assert n_groups + n_experts <= LANES
    slopes_m, slopes_d = _alibi_slopes(n_heads, n_heads)
    tm_in = _pick_tile(n, 1024)
    tm_out = _pick_tile(n, 256)
    tb = 256
    tm_c = _pick_tile(n, 256)
    l = 0

    x2 = x.reshape(n, d)
    projm, projd, gates = _inproj(x2, norm1_g[l], w_in[l].astype(BF16), b_gates[l], width, tm_in)
    o_m = _moba(projm, slopes_m, bsz, seq, n_heads)
    o_d = _dilated(projd, slopes_d, bsz, seq, n_heads)

    w_r = jnp.concatenate([w_group[l], w_expert[l]], axis=1)
    w_r = jnp.pad(w_r, ((0, 0), (0, LANES - w_r.shape[1])))
    wr_hi = w_r.astype(BF16)
    wr_lo = (w_r - wr_hi.astype(F32)).astype(BF16)
    b_r = jnp.pad(jnp.concatenate([b_group[l], b_expert[l]]),
                  (0, LANES - n_groups - n_experts)).reshape(1, LANES)
    x1, h2, rt = _outproj(o_m, o_d, gates, x2, w_out_moba[l].astype(BF16),
                          w_out_dil[l].astype(BF16), w_o[l].astype(BF16), norm2_g[l],
                          wr_hi, wr_lo, b_r, n_groups, per_group, tm_out)

    expert_id = rt[:, 2:2 + EXPERT_TOPK].astype(jnp.int32)
    tok_blocks, blk_e, nvalid, pos_blocks = _dispatch_tables(expert_id, n_experts, tb, tm_c)
    wgu = jnp.concatenate([w_gate[l], w_up[l]], axis=-1).astype(BF16)
    y = _experts(h2, tok_blocks, blk_e, nvalid, wgu, w_down[l].astype(BF16), tb)
    out = _combine(x1, rt, norm_f_g, y, pos_blocks, tm_c)
    return out.reshape(bsz, seq, d)
```

```python
import functools

import numpy as np
import jax
import jax.numpy as jnp
from jax import lax
from jax.experimental import pallas as pl
from jax.experimental.pallas import tpu as pltpu

HEAD_DIM = 128
MOBA_BLOCK = 256
MOBA_TOPK = 3
DIL_PATTERNS = ((128, 1), (512, 4), (2048, 16))
DIL_STEPS = 128
DIL_CHUNK = 2048
DIL_BATCH = 16
EXPERT_TOPK = 2
RMS_EPS = 1e-6
NEG = -1e30
LOG2E = 1.4426950408889634
LANES = 128
VMEM_LIMIT = 56 * 1024 * 1024

F32 = jnp.float32
BF16 = jnp.bfloat16


def _alibi_slopes(n_moba, n_dil):
    n = n_moba + n_dil
    s = 2.0 ** (-8.0 * np.arange(1, n + 1) / n)
    return jnp.asarray(s[0::2], F32), jnp.asarray(s[1::2], F32)


def _inproj_kernel(x_ref, g_ref, w_ref, b_ref, o_ref, h_scr, *, qscale, gates):
    j = pl.program_id(1)

    @pl.when(j == 0)
    def _():
        x = x_ref[...]
        ms = jnp.mean(x * x, axis=-1, keepdims=True)
        h_scr[...] = (x * lax.rsqrt(ms + RMS_EPS) * g_ref[...]).astype(BF16)

    acc = jnp.dot(h_scr[...], w_ref[...], preferred_element_type=F32)
    if gates:
        o_ref[...] = jax.nn.sigmoid(acc + b_ref[...]).astype(o_ref.dtype)
    else:
        o_ref[...] = (acc * jnp.where(j == 0, qscale, 1.0)).astype(o_ref.dtype)


def _inproj_part(x2, g1, w_in, bias, tm, tn, col0, n_tiles, dtype, gates):
    n, d = x2.shape
    qscale = HEAD_DIM ** -0.5 * LOG2E
    return pl.pallas_call(
        functools.partial(_inproj_kernel, qscale=qscale, gates=gates),
        grid=(n // tm, n_tiles),
        in_specs=[
            pl.BlockSpec((tm, d), lambda i, j: (i, 0)),
            pl.BlockSpec((1, d), lambda i, j: (0, 0)),
            pl.BlockSpec((d, tn), lambda i, j: (0, col0 + j)),
            pl.BlockSpec((1, tn), lambda i, j: (0, j)),
        ],
        out_specs=pl.BlockSpec((tm, tn), lambda i, j: (i, j)),
        out_shape=jax.ShapeDtypeStruct((n, n_tiles * tn), dtype),
        scratch_shapes=[pltpu.VMEM((tm, d), BF16)],
        compiler_params=pltpu.CompilerParams(
            dimension_semantics=("arbitrary", "arbitrary"), vmem_limit_bytes=VMEM_LIMIT),
        name="inproj_gates" if gates else "inproj_qkv",
    )(x2, g1.reshape(1, d), w_in, bias)


def _inproj(x2, g1, w_in, b_gates, width, tm):
    n, d = x2.shape
    tn = width
    assert w_in.shape[1] == 6 * width + 2 * d and d % tn == 0 and n % tm == 0
    n_gate = (2 * d) // tn
    zeros = jnp.zeros((1, 3 * width), F32)
    projm = _inproj_part(x2, g1, w_in, zeros, tm, tn, 0, 3, BF16, False)
    projd = _inproj_part(x2, g1, w_in, zeros, tm, tn, 3, 3, F32, False)
    gates = _inproj_part(x2, g1, w_in, b_gates.reshape(1, 2 * d), tm, tn, 6, n_gate, BF16, True)
    return projm, projd, gates


def _nt_dot(a, b):
    return lax.dot_general(a, b, (((1,), (1,)), ((), ())), preferred_element_type=F32)


def _dot(a, b):
    return jnp.dot(a, b, preferred_element_type=F32)


MOBA_TILE = 2 * MOBA_BLOCK
ALIBI_SPLIT = 3
VT_ROWS = HEAD_DIM + 16


def _moba_kernel(slopes_ref, q_ref, k_ref, v_ref, o_ref, kaug, vt, kmh, kml, causal, sbuf, pbuf,
                 *, nb, nbp, hpg):
    blk = MOBA_BLOCK
    hd = HEAD_DIM
    tq = tk = MOBA_TILE
    hg = pl.program_id(1)
    qp = pl.program_id(2)
    heads = range(hpg)
    slope2 = [slopes_ref[hg * hpg + g] * LOG2E for g in heads]
    cols = [slice(g * hd, (g + 1) * hd) for g in heads]
    s_len = k_ref.shape[0]
    c_hi0 = nbp + ALIBI_SPLIT

    @pl.when(qp == 0)
    def _init():
        pos = lax.broadcasted_iota(jnp.int32, (s_len, LANES), 0)
        lane = lax.broadcasted_iota(jnp.int32, (s_len, LANES), 1)
        off = pos % tk
        aug = jnp.where(lane < nbp, jnp.where(pos // blk == lane, 1.0, 0.0),
                        jnp.where(lane < c_hi0, (off % blk).astype(F32),
                                  jnp.where(lane < c_hi0 + ALIBI_SPLIT,
                                            (off - off % blk).astype(F32), 0.0))).astype(BF16)
        row = lax.broadcasted_iota(jnp.int32, (VT_ROWS - hd, s_len), 0)
        ones_row = jnp.where(row == 0, 1.0, 0.0).astype(BF16)
        for g in heads:
            kaug[g, :, 0:hd] = k_ref[:, cols[g]]
            kaug[g, :, hd:hd + LANES] = aug

            def xpose(n, carry, g=g):
                r0 = pl.multiple_of(n * blk, blk)
                vt[g, 0:hd, pl.ds(r0, blk)] = (
                    v_ref[pl.ds(r0, blk), cols[g]].astype(F32).T.astype(BF16))
                return carry

            lax.fori_loop(0, nb, xpose, 0)
            vt[g, hd:VT_ROWS, :] = ones_row
            km = jnp.sum(k_ref[:, cols[g]].astype(F32).reshape(nb, blk, hd), axis=1) * (1.0 / blk)
            if nb < nbp:
                km = jnp.concatenate([km, jnp.zeros((nbp - nb, hd), F32)], axis=0)
            hi = km.astype(BF16)
            kmh[g] = hi
            kml[g] = (km - hi.astype(F32)).astype(BF16)
        cc = lax.broadcasted_iota(jnp.int32, (tk, tq), 0)
        rr = lax.broadcasted_iota(jnp.int32, (tk, tq), 1)
        causal[...] = jnp.where(cc > rr, NEG, 0.0)

    lane_q = lax.broadcasted_iota(jnp.int32, (1, tq), 1)
    qblk = qp * (tq // blk) + lane_q // blk
    t_q = qp * tq + lane_q
    bidx = lax.broadcasted_iota(jnp.int32, (nbp, tq), 0)
    ri = lax.broadcasted_iota(jnp.int32, (16, tq), 0)
    tail = jnp.zeros((LANES - nbp - 16, tq), BF16)

    def augmented_queries(g):
        qt = q_ref[:, cols[g]].astype(F32).T.astype(BF16)
        gate = _dot(kmh[g], qt) + _dot(kml[g], qt)
        gt = jnp.where(bidx < qblk, gate, -jnp.inf)
        selneg = jnp.full((nbp, tq), NEG, F32)
        for _ in range(min(MOBA_TOPK, nb)):
            mx = jnp.max(gt, axis=0, keepdims=True)
            idx = jnp.min(jnp.where(gt == mx, bidx, nbp), axis=0, keepdims=True)
            hit = bidx == idx
            selneg = jnp.where(hit & (mx > -jnp.inf), 0.0, selneg)
            gt = jnp.where(hit, -jnp.inf, gt)
        sv = jnp.full((16, tq), slope2[g], F32)
        pieces = []
        for _ in range(ALIBI_SPLIT):
            pc = sv.astype(BF16).astype(F32)
            pieces.append(pc)
            sv = sv - pc
        srows = jnp.zeros((16, tq), F32)
        for j in range(2 * ALIBI_SPLIT):
            srows = jnp.where(ri == j, pieces[j % ALIBI_SPLIT], srows)

        def augment(sel):
            return jnp.concatenate([qt, sel.astype(BF16), srows.astype(BF16), tail], axis=0)

        return augment(selneg), augment(jnp.where(bidx == qblk, 0.0, selneg))

    qaugs = [augmented_queries(g) for g in heads]

    def scores(g, k0, qa):
        return _dot(kaug[g, pl.ds(k0, tk), :], qa)

    def softmax(g, k0, raw, m):
        shift = slope2[g] * (k0 - t_q).astype(F32)
        m_new = jnp.maximum(m, jnp.max(raw, axis=0, keepdims=True) + shift)
        return m_new, jnp.exp2(raw - (m_new - shift)), jnp.exp2(m - m_new)

    def values(g, k0):
        return _dot(vt[g, :, pl.ds(k0, tk)], pbuf[g])

    for g in heads:
        sbuf[g] = scores(g, pl.multiple_of(qp * tk, tk), qaugs[g][1]) + causal[...]
        pbuf[g] = jnp.zeros((tk, tq), BF16)

    def step(j, st):
        k_cur = pl.multiple_of((qp - j) * tk, tk)
        k_prev = pl.multiple_of(jnp.minimum(k_cur + tk, qp * tk), tk)
        raws = [scores(g, k_cur - tk, qaugs[g][0]) for g in heads]
        pvs = [values(g, k_prev) for g in heads]
        out = []
        for g in heads:
            m, alpha_prev, acc = st[g]
            m_new, p, alpha = softmax(g, k_cur, sbuf[g], m)
            acc = alpha_prev * acc + pvs[g]
            pbuf[g] = p.astype(BF16)
            sbuf[g] = raws[g]
            out.append((m_new, alpha, acc))
        return tuple(out)

    init = (jnp.full((1, tq), -jnp.inf, F32), jnp.ones((1, tq), F32),
            jnp.zeros((VT_ROWS, tq), F32))
    fin = lax.fori_loop(0, qp, step, tuple(init for _ in heads))
    k_last = pl.multiple_of(jnp.minimum(tk, qp * tk), tk)
    for g in heads:
        m, alpha_prev, acc = fin[g]
        acc = alpha_prev * acc + values(g, k_last)
        _, p, alpha = softmax(g, 0, sbuf[g], m)
        pbuf[g] = p.astype(BF16)
        acc = alpha * acc + values(g, 0)
        o_ref[:, cols[g]] = (acc[0:hd, :] / acc[hd:hd + 1, :]).T.astype(o_ref.dtype)


def _moba(projm, slopes, bsz, seq, n_heads):
    width = n_heads * HEAD_DIM
    nb = seq // MOBA_BLOCK
    nbp = -(-nb // 16) * 16
    tile = MOBA_TILE
    hpg = 2 if n_heads % 2 == 0 else 1
    ng = n_heads // hpg
    gw = hpg * HEAD_DIM
    assert seq % tile == 0 and nbp + 16 <= LANES
    pm = projm.reshape(bsz, seq, 3 * width)
    grid_spec = pltpu.PrefetchScalarGridSpec(
        num_scalar_prefetch=1,
        grid=(bsz, ng, seq // tile),
        in_specs=[
            pl.BlockSpec((None, tile, gw), lambda b, h, i, s: (b, i, h)),
            pl.BlockSpec((None, seq, gw), lambda b, h, i, s: (b, 0, ng + h)),
            pl.BlockSpec((None, seq, gw), lambda b, h, i, s: (b, 0, 2 * ng + h)),
        ],
        out_specs=pl.BlockSpec((None, tile, gw), lambda b, h, i, s: (b, i, h)),
        scratch_shapes=[
            pltpu.VMEM((hpg, seq, HEAD_DIM + LANES), BF16),
            pltpu.VMEM((hpg, VT_ROWS, seq), BF16),
            pltpu.VMEM((hpg, nbp, HEAD_DIM), BF16),
            pltpu.VMEM((hpg, nbp, HEAD_DIM), BF16),
            pltpu.VMEM((tile, tile), F32),
            pltpu.VMEM((hpg, tile, tile), F32),
            pltpu.VMEM((hpg, tile, tile), BF16),
        ],
    )
    out = pl.pallas_call(
        functools.partial(_moba_kernel, nb=nb, nbp=nbp, hpg=hpg),
        grid_spec=grid_spec,
        out_shape=jax.ShapeDtypeStruct((bsz, seq, width), BF16),
        compiler_params=pltpu.CompilerParams(
            dimension_semantics=("arbitrary", "arbitrary", "arbitrary"),
            vmem_limit_bytes=VMEM_LIMIT),
        name="moba",
    )(slopes, pm, pm, pm)
    return out.reshape(bsz * seq, width)


def _dil_kernel(slopes_ref, q_ref, kc_ref, kp_ref, vc_ref, vp_ref, o_ref, bias, osc, lsc):
    w = DIL_STEPS
    chunk = q_ref.shape[0]
    h = pl.program_id(1)
    c = pl.program_id(2)
    slope2 = slopes_ref[h] * LOG2E

    @pl.when(c == 0)
    def _init():
        i = lax.broadcasted_iota(jnp.int32, (w, 2 * w), 0)
        j = lax.broadcasted_iota(jnp.int32, (w, 2 * w), 1)
        diff = i + w - j
        ok = (diff >= 0) & (diff <= w)
        for p, (_, d) in enumerate(DIL_PATTERNS):
            bias[p] = jnp.where(ok, diff.astype(F32) * (-slope2 * d), NEG)

    jj = lax.broadcasted_iota(jnp.int32, (w, 2 * w), 1)
    first_pen = jnp.where(jj < w, jnp.where(c > 0, 0.0, NEG), 0.0)

    def rows(start, d):
        return pl.ds(start, w) if d == 1 else pl.ds(start, w, stride=d)

    def tiles(p, d, qstarts, kprev_ref, vprev_ref, pstarts, from_prev_chunk):
        n = len(qstarts)
        ss = []
        for qstart, pstart in zip(qstarts, pstarts):
            q = q_ref[rows(qstart, d), :].astype(BF16)
            k2 = jnp.concatenate([kprev_ref[rows(pstart, d), :], kc_ref[rows(qstart, d), :]],
                                 axis=0).astype(BF16)
            ss.append(_nt_dot(q, k2))
        prs, ms, ls = [], [], []
        for s in ss:
            s = s + bias[p]
            if from_prev_chunk:
                s = s + first_pen
            m = jnp.max(s, axis=-1, keepdims=True)
            pr = jnp.exp2(s - m)
            prs.append(pr.astype(BF16))
            ms.append(m)
            ls.append(jnp.sum(pr, axis=-1, keepdims=True))
        for i in range(n):
            v2 = jnp.concatenate([vprev_ref[rows(pstarts[i], d), :],
                                  vc_ref[rows(qstarts[i], d), :]], axis=0).astype(BF16)
            o = jnp.dot(prs[i], v2, preferred_element_type=F32) / ls[i]
            osc[p, rows(qstarts[i], d), :] = o
            lsc[p, rows(qstarts[i], d), :] = jnp.broadcast_to(ms[i] + jnp.log2(ls[i]), (w, LANES))

    def batch_size(n):
        return max(g for g in range(1, DIL_BATCH + 1) if n % g == 0)

    for p, (_, d) in enumerate(DIL_PATTERNS):
        span = w * d
        n_sb = chunk // span

        g1 = batch_size(d)

        def first(t, carry, p=p, d=d, span=span, g1=g1):
            rs = [t * g1 + i for i in range(g1)]
            tiles(p, d, rs, kp_ref, vp_ref, [chunk - span + r for r in rs], True)
            return carry

        lax.fori_loop(0, d // g1, first, 0)

        n_rest = (n_sb - 1) * d
        if n_rest:
            g2 = batch_size(n_rest)

            def rest(t, carry, p=p, d=d, span=span, g2=g2):
                idx = [t * g2 + i for i in range(g2)]
                qstarts = [(1 + x // d) * span + x % d for x in idx]
                if d == 1:
                    qstarts = [pl.multiple_of(x, w) for x in qstarts]
                tiles(p, d, qstarts, kc_ref, vc_ref, [x - span for x in qstarts], False)
                return carry

            lax.fori_loop(0, n_rest // g2, rest, 0)

    rb = 256

    def merge(t, carry):
        r0 = pl.multiple_of(t * rb, rb)
        ls = [lsc[p, pl.ds(r0, rb), :] for p in range(len(DIL_PATTERNS))]
        mx = functools.reduce(jnp.maximum, ls)
        ws = [jnp.exp2(x - mx) for x in ls]
        num = ws[0] * osc[0, pl.ds(r0, rb), :]
        den = ws[0]
        for p in range(1, len(DIL_PATTERNS)):
            num = num + ws[p] * osc[p, pl.ds(r0, rb), :]
            den = den + ws[p]
        o_ref[pl.ds(r0, rb), :] = (num / den).astype(o_ref.dtype)
        return carry

    lax.fori_loop(0, chunk // rb, merge, 0)


def _dilated(projd, slopes, bsz, seq, n_heads):
    width = n_heads * HEAD_DIM
    chunk = DIL_CHUNK
    assert seq % chunk == 0
    assert all(win // d == DIL_STEPS and chunk % win == 0 for win, d in DIL_PATTERNS)
    pd_ = projd.reshape(bsz, seq, 3 * width)
    blk = (None, chunk, HEAD_DIM)
    npat = len(DIL_PATTERNS)
    grid_spec = pltpu.PrefetchScalarGridSpec(
        num_scalar_prefetch=1,
        grid=(bsz, n_heads, seq // chunk),
        in_specs=[
            pl.BlockSpec(blk, lambda b, h, c, s: (b, c, h)),
            pl.BlockSpec(blk, lambda b, h, c, s: (b, c, n_heads + h)),
            pl.BlockSpec(blk, lambda b, h, c, s: (b, jnp.maximum(c - 1, 0), n_heads + h)),
            pl.BlockSpec(blk, lambda b, h, c, s: (b, c, 2 * n_heads + h)),
            pl.BlockSpec(blk, lambda b, h, c, s: (b, jnp.maximum(c - 1, 0), 2 * n_heads + h)),
        ],
        out_specs=pl.BlockSpec(blk, lambda b, h, c, s: (b, c, h)),
        scratch_shapes=[
            pltpu.VMEM((npat, DIL_STEPS, 2 * DIL_STEPS), F32),
            pltpu.VMEM((npat, chunk, HEAD_DIM), F32),
            pltpu.VMEM((npat, chunk, LANES), F32),
        ],
    )
    out = pl.pallas_call(
        _dil_kernel,
        grid_spec=grid_spec,
        out_shape=jax.ShapeDtypeStruct((bsz, seq, width), BF16),
        compiler_params=pltpu.CompilerParams(
            dimension_semantics=("arbitrary", "arbitrary", "arbitrary"),
            vmem_limit_bytes=VMEM_LIMIT),
        name="dilated",
    )(slopes, pd_, pd_, pd_, pd_, pd_)
    return out.reshape(bsz * seq, width)


def _store_slabs(ref, val):
    rows, d = val.shape
    slabs = d // LANES
    for s in range(slabs):
        ref[pl.ds(s, rows, stride=slabs), :] = val[:, s * LANES:(s + 1) * LANES]


def _load_slab(ref3, slot, first_row, rows, slabs, s):
    return ref3[slot, pl.ds(first_row * slabs + s, rows, stride=slabs), :]


def _outproj_kernel(am_ref, ad_ref, gm_ref, gd_ref, x_ref, wm_ref, wd_ref, wo_ref, g2_ref,
                    wrh_ref, wrl_ref, br_ref, x1_ref, h2_ref, rt_ref, *, n_groups, per_group):
    y_m = jnp.dot(am_ref[...], wm_ref[...], preferred_element_type=F32)
    y_d = jnp.dot(ad_ref[...], wd_ref[...], preferred_element_type=F32)
    z = (gm_ref[...].astype(F32) * y_m + gd_ref[...].astype(F32) * y_d).astype(BF16)
    x1 = x_ref[...] + jnp.dot(z, wo_ref[...], preferred_element_type=F32)
    x1_ref[...] = x1
    ms = jnp.mean(x1 * x1, axis=-1, keepdims=True)
    h2 = x1 * lax.rsqrt(ms + RMS_EPS) * g2_ref[...]
    _store_slabs(h2_ref, h2)

    hi = h2.astype(BF16)
    lo = (h2 - hi.astype(F32)).astype(BF16)
    logits = (jnp.dot(hi, wrh_ref[...], preferred_element_type=F32)
              + (jnp.dot(hi, wrl_ref[...], preferred_element_type=F32)
                 + jnp.dot(lo, wrh_ref[...], preferred_element_type=F32))
              + br_ref[...])

    lane = lax.broadcasted_iota(jnp.int32, logits.shape, 1)
    big = jnp.int32(1 << 20)
    isg = lane < n_groups
    gl = jnp.where(isg, logits, -jnp.inf)
    gex = jnp.exp(gl - jnp.max(gl, axis=-1, keepdims=True))
    gprob = gex / jnp.sum(gex, axis=-1, keepdims=True)
    gtp = jnp.max(gprob, axis=-1, keepdims=True)
    gtop = jnp.min(jnp.where((gprob == gtp) & isg, lane, big), axis=-1, keepdims=True)
    e_lo = n_groups + gtop * per_group
    ise = (lane >= e_lo) & (lane < e_lo + per_group)
    el = jnp.where(ise, logits, -jnp.inf)
    eex = jnp.exp(el - jnp.max(el, axis=-1, keepdims=True))
    eprob = jnp.where(ise, eex / jnp.sum(eex, axis=-1, keepdims=True), -1.0)
    p1 = jnp.max(eprob, axis=-1, keepdims=True)
    i1 = jnp.min(jnp.where(eprob == p1, lane, big), axis=-1, keepdims=True)
    eprob2 = jnp.where(lane == i1, -1.0, eprob)
    p2 = jnp.max(eprob2, axis=-1, keepdims=True)
    i2 = jnp.min(jnp.where(eprob2 == p2, lane, big), axis=-1, keepdims=True)
    den = p1 + p2
    w1 = gtp * p1 / den
    w2 = gtp * p2 / den
    e1 = (i1 - n_groups).astype(F32)
    e2 = (i2 - n_groups).astype(F32)
    rt_ref[...] = jnp.where(lane == 0, w1, jnp.where(lane == 1, w2, jnp.where(
        lane == 2, e1, jnp.where(lane == 3, e2, 0.0))))


def _outproj(a_m, a_d, gates, x2, w_m, w_d, w_o, g2, wr_hi, wr_lo, b_r, n_groups, per_group, tm):
    n, d = x2.shape
    width = a_m.shape[1]
    assert n % tm == 0

    def const(shape):
        return pl.BlockSpec(shape, lambda i: (0, 0))

    return pl.pallas_call(
        functools.partial(_outproj_kernel, n_groups=n_groups, per_group=per_group),
        grid=(n // tm,),
        in_specs=[
            pl.BlockSpec((tm, width), lambda i: (i, 0)),
            pl.BlockSpec((tm, width), lambda i: (i, 0)),
            pl.BlockSpec((tm, d), lambda i: (i, 0)),
            pl.BlockSpec((tm, d), lambda i: (i, 1)),
            pl.BlockSpec((tm, d), lambda i: (i, 0)),
            const((width, d)), const((width, d)), const((d, d)), const((1, d)),
            const((d, LANES)), const((d, LANES)), const((1, LANES)),
        ],
        out_specs=[
            pl.BlockSpec((tm, d), lambda i: (i, 0)),
            pl.BlockSpec((tm * (d // LANES), LANES), lambda i: (i, 0)),
            pl.BlockSpec((tm, LANES), lambda i: (i, 0)),
        ],
        out_shape=[
            jax.ShapeDtypeStruct((n, d), F32),
            jax.ShapeDtypeStruct((n * (d // LANES), LANES), F32),
            jax.ShapeDtypeStruct((n, LANES), F32),
        ],
        compiler_params=pltpu.CompilerParams(
            dimension_semantics=("arbitrary",), vmem_limit_bytes=VMEM_LIMIT),
        name="outproj",
    )(a_m, a_d, gates, gates, x2, w_m, w_d, w_o, g2.reshape(1, d), wr_hi, wr_lo, b_r)


DMA_PRIORITIES = 2


def _start_row_gather(src_hbm, idx_ref, dst, sem, n_rows, slabs):
    def issue(t, carry):
        for k in range(DMA_PRIORITIES):
            r = t * DMA_PRIORITIES + k
            src_row = pl.multiple_of(idx_ref[0, 0, r] * slabs, slabs)
            pltpu.make_async_copy(src_hbm.at[pl.ds(src_row, slabs), :],
                                  dst.at[pl.ds(pl.multiple_of(r * slabs, slabs), slabs), :],
                                  sem).start(priority=k)
        return carry

    assert n_rows % DMA_PRIORITIES == 0
    lax.fori_loop(0, n_rows // DMA_PRIORITIES, issue, 0)


def _wait_row_gather(src_hbm, dst, sem, n_rows, slabs):
    pltpu.make_async_copy(src_hbm.at[pl.ds(0, n_rows * slabs), :], dst, sem).wait()


def _experts_kernel(blk_e_ref, nvalid_ref, tok_ref, tok_next_ref, h_hbm, wgu_ref, wd_ref, y_ref,
                    xbuf, sem, *, d_expert, tb):
    del blk_e_ref
    i = pl.program_id(0)
    nvalid = nvalid_ref[0]
    slabs = xbuf.shape[1] // tb
    slot = i % 2

    @pl.when((i == 0) & (i < nvalid))
    def _():
        _start_row_gather(h_hbm, tok_ref, xbuf.at[0], sem.at[0], tb, slabs)

    @pl.when(i + 1 < nvalid)
    def _():
        _start_row_gather(h_hbm, tok_next_ref, xbuf.at[1 - slot], sem.at[1 - slot], tb, slabs)

    @pl.when(i < nvalid)
    def _():
        _wait_row_gather(h_hbm, xbuf.at[slot], sem.at[slot], tb, slabs)
        x = jnp.concatenate([_load_slab(xbuf, slot, 0, tb, slabs, s).astype(BF16)
                             for s in range(slabs)], axis=1)
        gu = jnp.dot(x, wgu_ref[...], preferred_element_type=F32)
        a = (jax.nn.silu(gu[:, :d_expert]) * gu[:, d_expert:]).astype(BF16)
        _store_slabs(y_ref, jnp.dot(a, wd_ref[...], preferred_element_type=F32))

    @pl.when(i >= nvalid)
    def _():
        y_ref[...] = jnp.zeros_like(y_ref)


def _experts(h2s, tok_blocks, blk_e, nvalid, wgu, wd, tb):
    d = wgu.shape[1]
    slabs = d // LANES
    n_blk = tok_blocks.shape[0]
    d_expert = wd.shape[1]
    grid_spec = pltpu.PrefetchScalarGridSpec(
        num_scalar_prefetch=2,
        grid=(n_blk,),
        in_specs=[
            pl.BlockSpec((1, 1, tb), lambda i, be, nv: (i, 0, 0), memory_space=pltpu.SMEM),
            pl.BlockSpec((1, 1, tb), lambda i, be, nv: (jnp.minimum(i + 1, n_blk - 1), 0, 0),
                         memory_space=pltpu.SMEM),
            pl.BlockSpec(memory_space=pl.ANY),
            pl.BlockSpec((None, d, 2 * d_expert), lambda i, be, nv: (be[i], 0, 0)),
            pl.BlockSpec((None, d_expert, d), lambda i, be, nv: (be[i], 0, 0)),
        ],
        out_specs=pl.BlockSpec((tb * slabs, LANES), lambda i, be, nv: (i, 0)),
        scratch_shapes=[
            pltpu.VMEM((2, tb * slabs, LANES), F32),
            pltpu.SemaphoreType.DMA((2,)),
        ],
    )
    return pl.pallas_call(
        functools.partial(_experts_kernel, d_expert=d_expert, tb=tb),
        grid_spec=grid_spec,
        out_shape=jax.ShapeDtypeStruct((n_blk * tb * slabs, LANES), F32),
        compiler_params=pltpu.CompilerParams(
            dimension_semantics=("arbitrary",), vmem_limit_bytes=VMEM_LIMIT),
        name="experts",
    )(blk_e, nvalid, tok_blocks, tok_blocks, h2s, wgu, wd)


def _combine_kernel(pos_ref, pos_next_ref, x1_ref, rt_ref, gf_ref, y_hbm, o_ref, ybuf, sem):
    i = pl.program_id(0)
    n_steps = pl.num_programs(0)
    tm = x1_ref.shape[0]
    slot = i % 2

    slabs = x1_ref.shape[1] // LANES

    @pl.when(i == 0)
    def _():
        _start_row_gather(y_hbm, pos_ref, ybuf.at[0], sem.at[0], 2 * tm, slabs)

    @pl.when(i + 1 < n_steps)
    def _():
        _start_row_gather(y_hbm, pos_next_ref, ybuf.at[1 - slot], sem.at[1 - slot], 2 * tm, slabs)

    _wait_row_gather(y_hbm, ybuf.at[slot], sem.at[slot], 2 * tm, slabs)
    rt = rt_ref[...]
    x = jnp.concatenate(
        [x1_ref[:, s * LANES:(s + 1) * LANES]
         + (rt[:, 0:1] * _load_slab(ybuf, slot, 0, tm, slabs, s)
            + rt[:, 1:2] * _load_slab(ybuf, slot, tm, tm, slabs, s))
         for s in range(slabs)], axis=1)
    ms = jnp.mean(x * x, axis=-1, keepdims=True)
    o_ref[...] = x * lax.rsqrt(ms + RMS_EPS) * gf_ref[...]


def _combine(x1, rt, gf, y, pos_blocks, tm):
    n, d = x1.shape
    n_steps = n // tm
    return pl.pallas_call(
        _combine_kernel,
        grid=(n_steps,),
        in_specs=[
            pl.BlockSpec((1, 1, 2 * tm), lambda i: (i, 0, 0), memory_space=pltpu.SMEM),
            pl.BlockSpec((1, 1, 2 * tm), lambda i: (jnp.minimum(i + 1, n_steps - 1), 0, 0),
                         memory_space=pltpu.SMEM),
            pl.BlockSpec((tm, d), lambda i: (i, 0)),
            pl.BlockSpec((tm, LANES), lambda i: (i, 0)),
            pl.BlockSpec((1, d), lambda i: (0, 0)),
            pl.BlockSpec(memory_space=pl.ANY),
        ],
        out_specs=pl.BlockSpec((tm, d), lambda i: (i, 0)),
        out_shape=jax.ShapeDtypeStruct((n, d), F32),
        scratch_shapes=[
            pltpu.VMEM((2, 2 * tm * (d // LANES), LANES), F32),
            pltpu.SemaphoreType.DMA((2,)),
        ],
        compiler_params=pltpu.CompilerParams(
            dimension_semantics=("arbitrary",), vmem_limit_bytes=VMEM_LIMIT),
        name="combine",
    )(pos_blocks, pos_blocks, x1, rt, gf.reshape(1, d), y)


def _dispatch_tables(expert_id, n_experts, tb, tm):
    n = expert_id.shape[0]
    a = n * EXPERT_TOPK
    e_flat = expert_id.reshape(a)
    tok_flat = jnp.arange(a, dtype=jnp.int32) // EXPERT_TOPK
    order = jnp.argsort(e_flat)
    e_sorted = e_flat[order]
    counts = jnp.bincount(e_flat, length=n_experts).astype(jnp.int32)
    padded = ((counts + tb - 1) // tb) * tb
    start = jnp.cumsum(counts) - counts
    pend = jnp.cumsum(padded)
    pstart = pend - padded
    dest = pstart[e_sorted] + (jnp.arange(a, dtype=jnp.int32) - start[e_sorted])
    n_blk = (a + n_experts * tb) // tb
    blk_start = jnp.arange(n_blk, dtype=jnp.int32) * tb
    blk_e = jnp.clip(jnp.sum(pend[None, :] <= blk_start[:, None], axis=1), 0,
                     n_experts - 1).astype(jnp.int32)
    nvalid = (pend[-1] // tb).astype(jnp.int32).reshape(1)
    row = jnp.arange(n_blk * tb, dtype=jnp.int32)
    e_row = jnp.repeat(blk_e, tb)
    i_in = row - pstart[e_row]
    src = jnp.clip(start[e_row] + i_in, 0, a - 1)
    buf_tok = jnp.where((i_in >= 0) & (i_in < counts[e_row]), tok_flat[order][src], 0)
    pos = dest[jnp.argsort(order)]
    tok_blocks = buf_tok.reshape(n_blk, 1, tb)
    pos_blocks = pos.reshape(n // tm, tm, EXPERT_TOPK).transpose(0, 2, 1).reshape(
        n // tm, 1, EXPERT_TOPK * tm)
    return tok_blocks, blk_e, nvalid, pos_blocks


def _pick_tile(n, pref):
    t = pref
    while n % t:
        t //= 2
    return t


def kernel(x, norm1_g, w_in, b_gates, w_out_moba, w_out_dil, w_o, norm2_g, w_group, b_group,
           w_expert, b_expert, w_gate, w_up, w_down, norm_f_g):
    bsz, seq, d = x.shape
    n = bsz * seq
    assert w_in.shape[0] == 1, "one layer: the final RMSNorm is fused into the combine stage"
    width = w_out_moba.shape[1]
    assert w_out_dil.shape[1] == width and width % HEAD_DIM == 0
    n_heads = width // HEAD_DIM
    n_groups = w_group.shape[-1]
    n_experts = w_expert.shape[-1]
    per_group = n_experts // n_groups
    assert n_groups + n_experts <= LANES
    slopes_m, slopes_d = _alibi_slopes(n_heads, n_heads)
    tm_in = _pick_tile(n, 1024)
    tm_out = _pick_tile(n, 256)
    tb = 256
    tm_c = _pick_tile(n, 256)
    l = 0

    x2 = x.reshape(n, d)
    projm, projd, gates = _inproj(x2, norm1_g[l], w_in[l].astype(BF16), b_gates[l], width, tm_in)
    o_m = _moba(projm, slopes_m, bsz, seq, n_heads)
    o_d = _dilated(projd, slopes_d, bsz, seq, n_heads)

    w_r = jnp.concatenate([w_group[l], w_expert[l]], axis=1)
    w_r = jnp.pad(w_r, ((0, 0), (0, LANES - w_r.shape[1])))
    wr_hi = w_r.astype(BF16)
    wr_lo = (w_r - wr_hi.astype(F32)).astype(BF16)
    b_r = jnp.pad(jnp.concatenate([b_group[l], b_expert[l]]),
                  (0, LANES - n_groups - n_experts)).reshape(1, LANES)
    x1, h2, rt = _outproj(o_m, o_d, gates, x2, w_out_moba[l].astype(BF16),
                          w_out_dil[l].astype(BF16), w_o[l].astype(BF16), norm2_g[l],
                          wr_hi, wr_lo, b_r, n_groups, per_group, tm_out)

    expert_id = rt[:, 2:2 + EXPERT_TOPK].astype(jnp.int32)
    tok_blocks, blk_e, nvalid, pos_blocks = _dispatch_tables(expert_id, n_experts, tb, tm_c)
    wgu = jnp.concatenate([w_gate[l], w_up[l]], axis=-1).astype(BF16)
    y = _experts(h2, tok_blocks, blk_e, nvalid, wgu, w_down[l].astype(BF16), tb)
    out = _combine(x1, rt, norm_f_g, y, pos_blocks, tm_c)
    return out.reshape(bsz, seq, d)
```
